```python
import jax
import jax.numpy as jnp
from jax import lax
import numpy as np

D_MODEL = 2048
BATCH = 4
SEQ = 4096
DEPTH = 2

N_HEADS = 16
HEAD_DIM = D_MODEL // N_HEADS
D_ATT = N_HEADS * HEAD_DIM
D_CONV = D_MODEL
D_MIX = D_ATT + D_CONV
D_IN = 4 * D_ATT + 3 * D_CONV
CONV_WIDTH = 31
DILATED_PATTERNS = ((128, 1), (512, 4), (2048, 16))
Q_BLOCK = 128
ROPE_THETA = 10000.0
EPS = 1e-6
SPLIT_POINTS = (D_ATT, 2 * D_ATT, 3 * D_ATT, 4 * D_ATT, 4 * D_ATT + D_CONV, 4 * D_ATT + 2 * D_CONV)

kernel_name = "hybrid_dilated_attn_conformer_conv"


def rms_norm(x, gain):
    xf = x.astype(jnp.float32)
    y = xf * lax.rsqrt(jnp.mean(xf * xf, axis=-1, keepdims=True) + EPS)
    return (y * gain.astype(jnp.float32)).astype(x.dtype)


def layer_norm(x, gain, bias):
    xf = x.astype(jnp.float32)
    xc = xf - jnp.mean(xf, axis=-1, keepdims=True)
    y = xc * lax.rsqrt(jnp.mean(xc * xc, axis=-1, keepdims=True) + EPS)
    return (y * gain.astype(jnp.float32) + bias.astype(jnp.float32)).astype(x.dtype)


def rope_tables(seq):
    inv_freq = 1.0 / (ROPE_THETA ** (jnp.arange(0, HEAD_DIM, 2, dtype=jnp.float32) / HEAD_DIM))
    ang = jnp.arange(seq, dtype=jnp.float32)[:, None] * inv_freq[None, :]
    return jnp.cos(ang), jnp.sin(ang)


def apply_rope(x, cos, sin):
    x1, x2 = jnp.split(x.astype(jnp.float32), 2, axis=-1)
    c = cos[None, :, None, :]
    s = sin[None, :, None, :]
    return jnp.concatenate([x1 * c - x2 * s, x2 * c + x1 * s], axis=-1).astype(x.dtype)


def dilated_branch(q, k, v, window, dilation):
    B, S, H, Dh = q.shape
    L = S // dilation
    n_back = window // dilation
    assert n_back <= Q_BLOCK
    Lp = -(-L // Q_BLOCK) * Q_BLOCK
    nb = Lp // Q_BLOCK

    def strided(t):
        t = t.reshape(B, L, dilation, H, Dh).transpose(0, 2, 1, 3, 4)
        t = jnp.pad(t, ((0, 0), (0, 0), (0, Lp - L), (0, 0), (0, 0)))
        return t.reshape(B, dilation, nb, Q_BLOCK, H, Dh)

    def with_prev(t):
        prev = jnp.pad(t[:, :, :-1], ((0, 0), (0, 0), (1, 0), (0, 0), (0, 0), (0, 0)))
        return jnp.concatenate([prev, t], axis=3)

    qb = strided(q)
    kb = with_prev(strided(k))
    vb = with_prev(strided(v))
    s = jnp.einsum("brnqhd,brnkhd->brnhqk", qb, kb, preferred_element_type=jnp.float32) * (Dh ** -0.5)
    blk = jnp.arange(nb)[:, None, None]
    qi = blk * Q_BLOCK + jnp.arange(Q_BLOCK)[None, :, None]
    ki = (blk - 1) * Q_BLOCK + jnp.arange(2 * Q_BLOCK)[None, None, :]
    dist = qi - ki
    mask = (dist >= 0) & (dist <= n_back) & (ki >= 0)
    s = jnp.where(mask[None, None, :, None], s, -jnp.inf)
    m = jnp.max(s, axis=-1)
    p = jnp.exp(s - m[..., None])
    l = jnp.sum(p, axis=-1)
    o = jnp.einsum("brnhqk,brnkhd->brnqhd", p.astype(v.dtype), vb, preferred_element_type=jnp.float32)

    def unstride(t):
        tail = t.shape[5:]
        t = t.reshape((B, dilation, Lp, H) + tail)[:, :, :L]
        t = jnp.moveaxis(t, 1, 2)
        return t.reshape((B, S, H) + tail)

    return unstride(o), unstride(jnp.moveaxis(m, 3, 4)), unstride(jnp.moveaxis(l, 3, 4))


def dilated_mixture(q, k, v):
    branches = [dilated_branch(q, k, v, w, d) for (w, d) in DILATED_PATTERNS]
    ms = jnp.stack([b[1] for b in branches])
    ls = jnp.stack([b[2] for b in branches])
    wts = jnp.exp(ms - jnp.max(ms, axis=0))
    num = wts[0][..., None] * branches[0][0]
    for i in range(1, len(branches)):
        num = num + wts[i][..., None] * branches[i][0]
    den = jnp.sum(wts * ls, axis=0)
    return (num / den[..., None]).astype(q.dtype)


def conformer_conv(glu_a, glu_b, dw_kernel, dw_bias, ln_g, ln_b, w_pw):
    u = glu_a * jax.nn.sigmoid(glu_b)
    u = jnp.pad(u, ((0, 0), (CONV_WIDTH - 1, 0), (0, 0)))
    y = lax.conv_general_dilated(u, dw_kernel.astype(u.dtype)[:, None, :], window_strides=(1,),
                                 padding="VALID", dimension_numbers=("NWC", "WIO", "NWC"),
                                 feature_group_count=D_CONV) + dw_bias
    y = jax.nn.silu(layer_norm(y, ln_g, ln_b))
    return jnp.einsum("bsc,ce->bse", y, w_pw)


def setup_inputs(seed: int = 0) -> dict:
    key = jax.random.key(seed)
    ks = jax.random.split(key, 13)
    nrm = jax.random.normal
    f32 = jnp.float32
    return {
        "x": nrm(ks[0], (BATCH, SEQ, D_MODEL), f32),
        "norm_g": 1.0 + 0.02 * nrm(ks[1], (DEPTH, D_MODEL), f32),
        "w_in": nrm(ks[2], (DEPTH, D_MODEL, D_IN), f32) * D_MODEL ** -0.5,
        "q_norm_g": 1.0 + 0.02 * nrm(ks[3], (DEPTH, HEAD_DIM), f32),
        "k_norm_g": 1.0 + 0.02 * nrm(ks[4], (DEPTH, HEAD_DIM), f32),
        "dw_kernel": nrm(ks[5], (DEPTH, CONV_WIDTH, D_CONV), f32) * CONV_WIDTH ** -0.5,
        "dw_bias": 0.02 * nrm(ks[6], (DEPTH, D_CONV), f32),
        "conv_ln_g": 1.0 + 0.02 * nrm(ks[7], (DEPTH, D_CONV), f32),
        "conv_ln_b": 0.02 * nrm(ks[8], (DEPTH, D_CONV), f32),
        "w_pw": nrm(ks[9], (DEPTH, D_CONV, D_CONV), f32) * D_CONV ** -0.5,
        "att_out_g": 1.0 + 0.02 * nrm(ks[10], (DEPTH, D_ATT), f32),
        "conv_out_g": 1.0 + 0.02 * nrm(ks[11], (DEPTH, D_CONV), f32),
        "w_out": nrm(ks[12], (DEPTH, D_MIX, D_MODEL), f32) * D_MIX ** -0.5,
    }


def reference(x, norm_g, w_in, q_norm_g, k_norm_g, dw_kernel, dw_bias, conv_ln_g, conv_ln_b, w_pw,
              att_out_g, conv_out_g, w_out):
    B, S, _ = x.shape
    cos, sin = rope_tables(S)
    for layer in range(DEPTH):
        h = rms_norm(x, norm_g[layer])
        proj = jnp.einsum("bsd,de->bse", h, w_in[layer])
        q, k, v, g_att, glu_a, glu_b, g_conv = jnp.split(proj, SPLIT_POINTS, axis=-1)
        q = apply_rope(rms_norm(q.reshape(B, S, N_HEADS, HEAD_DIM), q_norm_g[layer]), cos, sin)
        k = apply_rope(rms_norm(k.reshape(B, S, N_HEADS, HEAD_DIM), k_norm_g[layer]), cos, sin)
        v = v.reshape(B, S, N_HEADS, HEAD_DIM)
        att = dilated_mixture(q, k, v).reshape(B, S, D_ATT)
        att_y = rms_norm(att, att_out_g[layer]) * jax.nn.silu(g_att)
        conv = conformer_conv(glu_a, glu_b, dw_kernel[layer], dw_bias[layer], conv_ln_g[layer],
                              conv_ln_b[layer], w_pw[layer])
        conv_y = rms_norm(conv, conv_out_g[layer]) * jax.nn.silu(g_conv)
        y = jnp.einsum("bse,ed->bsd", jnp.concatenate([att_y, conv_y], axis=-1), w_out[layer])
        x = x + y
    return x
```

```python
import functools

import jax
import jax.numpy as jnp
from jax import lax
from jax.experimental import pallas as pl
from jax.experimental.pallas import tpu as pltpu

D_MODEL = 2048
N_HEADS = 16
HEAD_DIM = 128
D_ATT = N_HEADS * HEAD_DIM
D_CONV = D_MODEL
D_MIX = D_ATT + D_CONV
D_IN = 4 * D_ATT + 3 * D_CONV
CONV_WIDTH = 31
DILATIONS = (1, 4, 16)
Q_BLOCK = 128
ROPE_THETA = 10000.0
EPS = 1e-6
MASKED = -1e30

VMEM_LIMIT_BYTES = 56 * 1024 * 1024

F32 = jnp.float32
BF16 = jnp.bfloat16


def _silu(x):
    return x * jax.nn.sigmoid(x)


def _norm_proj_kernel(x_ref, g_ref, w_ref, o_ref, h_ref, *, row_chunk):
    tm = x_ref.shape[0]

    @pl.when(pl.program_id(1) == 0)
    def _():
        def body(i, carry):
            rows = pl.ds(pl.multiple_of(i * row_chunk, row_chunk), row_chunk)
            x = x_ref[rows, :]
            y = x * lax.rsqrt(jnp.mean(x * x, axis=-1, keepdims=True) + EPS)
            h_ref[rows, :] = (y * g_ref[...]).astype(BF16)
            return carry
        lax.fori_loop(0, tm // row_chunk, body, 0)

    o_ref[...] = jnp.dot(h_ref[...], w_ref[...], preferred_element_type=F32).astype(o_ref.dtype)


def _norm_proj(x2, g, w_bf, *, tm=1024, tn=1024):
    T = x2.shape[0]
    return pl.pallas_call(
        functools.partial(_norm_proj_kernel, row_chunk=128),
        out_shape=jax.ShapeDtypeStruct((T, D_IN), BF16),
        grid=(T // tm, D_IN // tn),
        in_specs=[
            pl.BlockSpec((tm, D_MODEL), lambda i, j: (i, 0)),
            pl.BlockSpec((1, D_MODEL), lambda i, j: (0, 0)),
            pl.BlockSpec((D_MODEL, tn), lambda i, j: (0, j)),
        ],
        out_specs=pl.BlockSpec((tm, tn), lambda i, j: (i, j)),
        scratch_shapes=[pltpu.VMEM((tm, D_MODEL), BF16)],
        compiler_params=pltpu.CompilerParams(
            dimension_semantics=("arbitrary", "arbitrary"), vmem_limit_bytes=VMEM_LIMIT_BYTES),
        name="norm_proj",
    )(x2, g, w_bf)


def _attn_kernel(q_ref, k_ref, v_ref, cos_ref, sin_ref, gq_ref, gk_ref, bias_ref, o_ref,
                 qs, ks, vs, t0, t1, acc_o, acc_m, acc_l, *, seq):
    chunk = 256
    quarter = seq // 4
    sixteenth = seq // 16

    def prepare(src_ref, g_ref, dst, normalize, scale):
        def body(i, carry):
            rows = pl.ds(pl.multiple_of(i * chunk, chunk), chunk)
            y = src_ref[rows, :].astype(F32)
            if normalize:
                y = y * lax.rsqrt(jnp.mean(y * y, axis=-1, keepdims=True) + EPS) * g_ref[...]
                y = y * cos_ref[rows, :] + pltpu.roll(y, HEAD_DIM // 2, 1) * sin_ref[rows, :]
                if scale != 1.0:
                    y = y * scale
            t0[rows, :] = y
            dst[0, rows, :] = y.astype(BF16)
            return carry
        lax.fori_loop(0, seq // chunk, body, 0)

        for r4 in range(4):
            def body4(i, carry, r4=r4):
                y = t0[pl.ds(r4 + i * (4 * chunk), chunk, stride=4), :]
                rows = pl.ds(pl.multiple_of(r4 * quarter + i * chunk, chunk), chunk)
                t1[rows, :] = y
                dst[1, rows, :] = y.astype(BF16)
                return carry
            lax.fori_loop(0, quarter // chunk, body4, 0)

        for rp in range(4):
            for r4 in range(4):
                y = t1[pl.ds(r4 * quarter + rp, sixteenth, stride=4), :]
                dst[2, pl.ds((4 * rp + r4) * sixteenth, sixteenth), :] = y.astype(BF16)

    prepare(q_ref, gq_ref, qs, True, HEAD_DIM ** -0.5)
    prepare(k_ref, gk_ref, ks, True, 1.0)
    prepare(v_ref, None, vs, False, 1.0)

    n_blocks = seq // Q_BLOCK
    for br, d in enumerate(DILATIONS):
        nb = n_blocks // d

        def block(c, carry, br=br, d=d, nb=nb):
            n = lax.rem(c, nb)
            first = n == 0
            q_row = pl.multiple_of(c * Q_BLOCK, Q_BLOCK)
            k_row = pl.multiple_of(jnp.where(first, c * Q_BLOCK, (c - 1) * Q_BLOCK), Q_BLOCK)
            qb = qs[br, pl.ds(q_row, Q_BLOCK), :]
            kb = ks[br, pl.ds(k_row, 2 * Q_BLOCK), :]
            vb = vs[br, pl.ds(k_row, 2 * Q_BLOCK), :]
            s = lax.dot_general(qb, kb, (((1,), (1,)), ((), ())), preferred_element_type=F32)
            s = s + bias_ref[jnp.where(first, 0, 1)]
            m = jnp.max(s, axis=-1, keepdims=True)
            p = jnp.exp(s - m)
            l = jnp.sum(p, axis=-1, keepdims=True)
            o = jnp.dot(p.astype(BF16), vb, preferred_element_type=F32)
            mb = jnp.broadcast_to(m, (Q_BLOCK, HEAD_DIM))
            lb = jnp.broadcast_to(l, (Q_BLOCK, HEAD_DIM))
            if d == 1:
                rows = pl.ds(q_row, Q_BLOCK)
                acc_o[rows, :] = o
                acc_m[rows, :] = mb
                acc_l[rows, :] = lb
            else:
                rows = pl.ds(c // nb + n * (Q_BLOCK * d), Q_BLOCK, stride=d)
                m_old = acc_m[rows, :]
                m_new = jnp.maximum(m_old, mb)
                a_old = jnp.exp(m_old - m_new)
                a_blk = jnp.exp(mb - m_new)
                acc_o[rows, :] = a_old * acc_o[rows, :] + a_blk * o
                acc_l[rows, :] = a_old * acc_l[rows, :] + a_blk * lb
                acc_m[rows, :] = m_new
            return carry
        lax.fori_loop(0, n_blocks, block, 0)

    def finish(i, carry):
        rows = pl.ds(pl.multiple_of(i * chunk, chunk), chunk)
        o_ref[rows, :] = (acc_o[rows, :] / acc_l[rows, :]).astype(o_ref.dtype)
        return carry
    lax.fori_loop(0, seq // chunk, finish, 0)


def _attention(proj3, cos2, sin2, gq, gk, bias):
    B, S, _ = proj3.shape
    head_spec = lambda off: pl.BlockSpec((None, S, HEAD_DIM), lambda b, h: (b, 0, off + h))
    const2 = lambda shape: pl.BlockSpec(shape, lambda b, h: (0,) * len(shape))
    return pl.pallas_call(
        functools.partial(_attn_kernel, seq=S),
        out_shape=jax.ShapeDtypeStruct((B, S, D_ATT), F32),
        grid=(B, N_HEADS),
        in_specs=[
            head_spec(0), head_spec(N_HEADS), head_spec(2 * N_HEADS),
            const2((S, HEAD_DIM)), const2((S, HEAD_DIM)),
            const2((1, HEAD_DIM)), const2((1, HEAD_DIM)),
            const2((2, Q_BLOCK, 2 * Q_BLOCK)),
        ],
        out_specs=pl.BlockSpec((None, S, HEAD_DIM), lambda b, h: (b, 0, h)),
        scratch_shapes=[
            pltpu.VMEM((3, S, HEAD_DIM), BF16), pltpu.VMEM((3, S, HEAD_DIM), BF16),
            pltpu.VMEM((3, S, HEAD_DIM), BF16),
            pltpu.VMEM((S, HEAD_DIM), F32), pltpu.VMEM((S, HEAD_DIM), F32),
            pltpu.VMEM((S, HEAD_DIM), F32), pltpu.VMEM((S, HEAD_DIM), F32), pltpu.VMEM((S, HEAD_DIM), F32),
        ],
        compiler_params=pltpu.CompilerParams(
            dimension_semantics=("arbitrary", "arbitrary"), vmem_limit_bytes=VMEM_LIMIT_BYTES),
        name="dilated_attention",
    )(proj3, proj3, proj3, cos2, sin2, gq, gk, bias)


CONV_HALO = 32


def _conv_kernel(a_ref, b_ref, gc_ref, dw_ref, dwb_ref, lng_ref, lnb_ref, wpw_ref, og_ref, o_ref,
                 u_ext, y_s, *, row_chunk):
    tm = a_ref.shape[0]

    @pl.when(pl.program_id(1) == 0)
    def _():
        u_ext[0:CONV_HALO, :] = jnp.zeros((CONV_HALO, D_CONV), F32)

    u_ext[CONV_HALO:CONV_HALO + tm, :] = a_ref[...].astype(F32) * jax.nn.sigmoid(b_ref[...].astype(F32))

    lead = CONV_HALO - (CONV_WIDTH - 1)

    def lanes_body(cc, carry):
        lanes = pl.ds(pl.multiple_of(cc * 128, 128), 128)
        for rc in range(tm // row_chunk):
            r0 = rc * row_chunk
            acc = jnp.broadcast_to(dwb_ref[:, lanes], (row_chunk, 128))
            for j in range(CONV_WIDTH):
                acc = acc + dw_ref[j:j + 1, lanes] * u_ext[r0 + lead + j:r0 + lead + j + row_chunk, lanes]
            y_s[r0:r0 + row_chunk, lanes] = acc
        return carry
    lax.fori_loop(0, D_CONV // 128, lanes_body, 0)

    u_ext[0:CONV_HALO, :] = u_ext[tm:tm + CONV_HALO, :]

    y = y_s[...]
    yc = y - jnp.mean(y, axis=-1, keepdims=True)
    yn = yc * lax.rsqrt(jnp.mean(yc * yc, axis=-1, keepdims=True) + EPS)
    z = _silu(yn * lng_ref[...] + lnb_ref[...]).astype(BF16)
    conv = jnp.dot(z, wpw_ref[...], preferred_element_type=F32)
    cn = conv * lax.rsqrt(jnp.mean(conv * conv, axis=-1, keepdims=True) + EPS) * og_ref[...]
    o_ref[...] = (cn * _silu(gc_ref[...].astype(F32))).astype(o_ref.dtype)


def _conv_module(proj3, dw, dwb, lng, lnb, wpw_bf, og, *, tm=256):
    B, S, _ = proj3.shape
    cb = D_CONV // D_CONV
    col = lambda blk: pl.BlockSpec((None, tm, D_CONV), lambda b, i: (b, i, blk))
    const2 = lambda shape: pl.BlockSpec(shape, lambda b, i: (0,) * len(shape))
    base = 4 * D_ATT // D_CONV
    return pl.pallas_call(
        functools.partial(_conv_kernel, row_chunk=64),
        out_shape=jax.ShapeDtypeStruct((B, S, D_CONV), BF16),
        grid=(B, S // tm),
        in_specs=[
            col(base), col(base + cb), col(base + 2 * cb),
            const2((CONV_WIDTH, D_CONV)), const2((1, D_CONV)), const2((1, D_CONV)), const2((1, D_CONV)),
            const2((D_CONV, D_CONV)), const2((1, D_CONV)),
        ],
        out_specs=pl.BlockSpec((None, tm, D_CONV), lambda b, i: (b, i, 0)),
        scratch_shapes=[pltpu.VMEM((tm + CONV_HALO, D_CONV), F32), pltpu.VMEM((tm, D_CONV), F32)],
        compiler_params=pltpu.CompilerParams(
            dimension_semantics=("arbitrary", "arbitrary"), vmem_limit_bytes=VMEM_LIMIT_BYTES),
        name="conformer_conv",
    )(proj3, proj3, proj3, dw, dwb, lng, lnb, wpw_bf, og)


def _out_kernel(att_ref, ga_ref, cy_ref, x_ref, g_ref, w_ref, o_ref, lhs, *, row_chunk):
    tm = att_ref.shape[0]

    @pl.when(pl.program_id(1) == 0)
    def _():
        def body(i, carry):
            rows = pl.ds(pl.multiple_of(i * row_chunk, row_chunk), row_chunk)
            a = att_ref[rows, :]
            an = a * lax.rsqrt(jnp.mean(a * a, axis=-1, keepdims=True) + EPS) * g_ref[...]
            lhs[rows, 0:D_ATT] = (an * _silu(ga_ref[rows, :].astype(F32))).astype(BF16)
            lhs[rows, D_ATT:D_MIX] = cy_ref[rows, :]
            return carry
        lax.fori_loop(0, tm // row_chunk, body, 0)

    o_ref[...] = x_ref[...] + jnp.dot(lhs[...], w_ref[...], preferred_element_type=F32)


def _out_proj(att2, proj2, convy2, x2, g, w_bf, *, tm=512, tn=512):
    T = x2.shape[0]
    gate_blk = 3 * D_ATT // D_ATT
    return pl.pallas_call(
        functools.partial(_out_kernel, row_chunk=128),
        out_shape=jax.ShapeDtypeStruct((T, D_MODEL), F32),
        grid=(T // tm, D_MODEL // tn),
        in_specs=[
            pl.BlockSpec((tm, D_ATT), lambda i, j: (i, 0)),
            pl.BlockSpec((tm, D_ATT), lambda i, j: (i, gate_blk)),
            pl.BlockSpec((tm, D_CONV), lambda i, j: (i, 0)),
            pl.BlockSpec((tm, tn), lambda i, j: (i, j)),
            pl.BlockSpec((1, D_ATT), lambda i, j: (0, 0)),
            pl.BlockSpec((D_MIX, tn), lambda i, j: (0, j)),
        ],
        out_specs=pl.BlockSpec((tm, tn), lambda i, j: (i, j)),
        scratch_shapes=[pltpu.VMEM((tm, D_MIX), BF16)],
        compiler_params=pltpu.CompilerParams(
            dimension_semantics=("arbitrary", "arbitrary"), vmem_limit_bytes=VMEM_LIMIT_BYTES),
        name="out_proj",
    )(att2, proj2, convy2, x2, g, w_bf)


def _rope_tables(seq):
    inv_freq = 1.0 / (ROPE_THETA ** (jnp.arange(0, HEAD_DIM, 2, dtype=F32) / HEAD_DIM))
    ang = jnp.arange(seq, dtype=F32)[:, None] * inv_freq[None, :]
    cos, sin = jnp.cos(ang), jnp.sin(ang)
    return jnp.concatenate([cos, cos], axis=-1), jnp.concatenate([-sin, sin], axis=-1)


def _mask_bias():
    iq = jnp.arange(Q_BLOCK)[:, None]
    ik = jnp.arange(2 * Q_BLOCK)[None, :]
    def bias(shift):
        dist = iq + shift - ik
        return jnp.where((dist >= 0) & (dist <= Q_BLOCK), 0.0, MASKED).astype(F32)
    return jnp.stack([bias(0), bias(Q_BLOCK)])


def kernel(x, norm_g, w_in, q_norm_g, k_norm_g, dw_kernel, dw_bias, conv_ln_g, conv_ln_b, w_pw,
           att_out_g, conv_out_g, w_out):
    B, S, D = x.shape
    assert D == D_MODEL and S % (16 * 2 * Q_BLOCK) == 0
    depth = norm_g.shape[0]
    cos2, sin2 = _rope_tables(S)
    bias = _mask_bias()
    row = lambda v: v.reshape(1, -1)
    x2 = x.reshape(B * S, D)
    for layer in range(depth):
        proj2 = _norm_proj(x2, row(norm_g[layer]), w_in[layer].astype(BF16))
        proj3 = proj2.reshape(B, S, D_IN)
        att = _attention(proj3, cos2, sin2, row(q_norm_g[layer]), row(k_norm_g[layer]), bias)
        conv_y = _conv_module(proj3, dw_kernel[layer], row(dw_bias[layer]), row(conv_ln_g[layer]),
                              row(conv_ln_b[layer]), w_pw[layer].astype(BF16), row(conv_out_g[layer]))
        x2 = _out_proj(att.reshape(B * S, D_ATT), proj2, conv_y.reshape(B * S, D_CONV), x2,
                       row(att_out_g[layer]), w_out[layer].astype(BF16))
    return x2.reshape(B, S, D)
```

```python
import functools

import jax
import jax.numpy as jnp
from jax import lax
from jax.experimental import pallas as pl
from jax.experimental.pallas import tpu as pltpu

D_MODEL = 2048
N_HEADS = 16
HEAD_DIM = 128
D_ATT = N_HEADS * HEAD_DIM
D_CONV = D_MODEL
D_MIX = D_ATT + D_CONV
D_IN = 4 * D_ATT + 3 * D_CONV
CONV_WIDTH = 31
DILATIONS = (1, 4, 16)
Q_BLOCK = 128
ROPE_THETA = 10000.0
EPS = 1e-6
MASKED = -1e30

VMEM_LIMIT_BYTES = 56 * 1024 * 1024

F32 = jnp.float32
BF16 = jnp.bfloat16


def _silu(x):
    return x * jax.nn.sigmoid(x)


def _norm_proj_kernel(x_ref, g_ref, w_ref, o_ref, h_ref, *, row_chunk):
    tm = x_ref.shape[0]

    @pl.when(pl.program_id(1) == 0)
    def _():
        def body(i, carry):
            rows = pl.ds(pl.multiple_of(i * row_chunk, row_chunk), row_chunk)
            x = x_ref[rows, :]
            y = x * lax.rsqrt(jnp.mean(x * x, axis=-1, keepdims=True) + EPS)
            h_ref[rows, :] = (y * g_ref[...]).astype(BF16)
            return carry
        lax.fori_loop(0, tm // row_chunk, body, 0)

    o_ref[...] = jnp.dot(h_ref[...], w_ref[...], preferred_element_type=F32).astype(o_ref.dtype)


def _norm_proj(x2, g, w_bf, *, tm=1024, tn=1024):
    T = x2.shape[0]
    return pl.pallas_call(
        functools.partial(_norm_proj_kernel, row_chunk=128),
        out_shape=jax.ShapeDtypeStruct((T, D_IN), BF16),
        grid=(T // tm, D_IN // tn),
        in_specs=[
            pl.BlockSpec((tm, D_MODEL), lambda i, j: (i, 0)),
            pl.BlockSpec((1, D_MODEL), lambda i, j: (0, 0)),
            pl.BlockSpec((D_MODEL, tn), lambda i, j: (0, j)),
        ],
        out_specs=pl.BlockSpec((tm, tn), lambda i, j: (i, j)),
        scratch_shapes=[pltpu.VMEM((tm, D_MODEL), BF16)],
        compiler_params=pltpu.CompilerParams(
            dimension_semantics=("arbitrary", "arbitrary"), vmem_limit_bytes=VMEM_LIMIT_BYTES),
        name="norm_proj",
    )(x2, g, w_bf)


NAT, MOD4, MOD16 = 0, 1, 2


def _attn_kernel(q_ref, k_ref, v_ref, cos_ref, sin_ref, gq_ref, gk_ref, bias_ref, o_ref,
                 qs, ks, vs, t0, t1, *stats, seq, group):
    obuf, lbuf, mbuf = stats[0:3], stats[3:6], stats[6:9]
    chunk = 256
    quarter = seq // 4
    sixteenth = seq // 16

    @pl.when((pl.program_id(0) == 0) & (pl.program_id(1) == 0))
    def _():
        for layout in range(3):
            vs[layout, :, HEAD_DIM:2 * HEAD_DIM] = jnp.ones((seq, HEAD_DIM), BF16)

    def prepare(src_ref, g_ref, dst, first_layout, normalize, scale):
        def body(i, carry):
            rows = pl.ds(pl.multiple_of(i * chunk, chunk), chunk)
            y = src_ref[rows, :].astype(F32)
            if normalize:
                y = y * lax.rsqrt(jnp.mean(y * y, axis=-1, keepdims=True) + EPS) * g_ref[...]
                y = y * cos_ref[rows, :] + pltpu.roll(y, HEAD_DIM // 2, 1) * sin_ref[rows, :]
                if scale != 1.0:
                    y = y * scale
            t0[rows, :] = y
            if first_layout == NAT:
                dst[NAT, rows, 0:HEAD_DIM] = y.astype(BF16)
            return carry
        lax.fori_loop(0, seq // chunk, body, 0, unroll=4)

        for r4 in range(4):
            def body4(i, carry, r4=r4):
                y = t0[pl.ds(r4 + i * (4 * chunk), chunk, stride=4), :]
                rows = pl.ds(pl.multiple_of(r4 * quarter + i * chunk, chunk), chunk)
                t1[rows, :] = y
                dst[MOD4 - first_layout, rows, 0:HEAD_DIM] = y.astype(BF16)
                return carry
            lax.fori_loop(0, quarter // chunk, body4, 0)

        for rp in range(4):
            for r4 in range(4):
                y = t1[pl.ds(r4 * quarter + rp, sixteenth, stride=4), :]
                dst[MOD16 - first_layout, pl.ds((4 * rp + r4) * sixteenth, sixteenth), 0:HEAD_DIM] = y.astype(BF16)

    prepare(q_ref, gq_ref, qs, MOD4, True, HEAD_DIM ** -0.5)
    prepare(k_ref, gk_ref, ks, NAT, True, 1.0)
    prepare(v_ref, None, vs, NAT, False, 1.0)

    n_blocks = seq // Q_BLOCK
    piece = Q_BLOCK // 4
    for br, d in enumerate(DILATIONS):
        nb = n_blocks // d

        def one_block(c, br=br, d=d, nb=nb):
            n = lax.rem(c, nb)
            first = n == 0
            q_row = pl.multiple_of(c * Q_BLOCK, Q_BLOCK)
            k_row = pl.multiple_of(jnp.where(first, c * Q_BLOCK, (c - 1) * Q_BLOCK), Q_BLOCK)
            if d == 1:
                runs = [pl.ds(pl.multiple_of(r4 * quarter + c * piece, piece), piece) for r4 in range(4)]
                qb = jnp.concatenate([qs[0, run, :] for run in runs], axis=0)
            else:
                qb = qs[br - 1, pl.ds(q_row, Q_BLOCK), :]
            kb = ks[br, pl.ds(k_row, 2 * Q_BLOCK), :]
            vb = vs[br, pl.ds(k_row, 2 * Q_BLOCK), :]
            s = lax.dot_general(qb, kb, (((1,), (1,)), ((), ())), preferred_element_type=F32)
            s = s + bias_ref[jnp.where(first, 0, 1) + (0 if d == 1 else 2)]
            m = jnp.max(s, axis=-1, keepdims=True)
            p = jnp.exp(s - m).astype(BF16)
            oe = jnp.dot(p, vb, preferred_element_type=F32)
            mb = jnp.broadcast_to(m, (Q_BLOCK, HEAD_DIM))
            ob, lb = oe[:, 0:HEAD_DIM], oe[:, HEAD_DIM:2 * HEAD_DIM]
            if d == 1:
                for r4, run in enumerate(runs):
                    sub = slice(r4 * piece, (r4 + 1) * piece)
                    obuf[br][run, :] = ob[sub, :]
                    lbuf[br][run, :] = lb[sub, :]
                    mbuf[br][run, :] = mb[sub, :]
            else:
                if d == 4:
                    rows = pl.ds(q_row, Q_BLOCK)
                else:
                    r = c // nb
                    rows = pl.ds((r % 4) * quarter + r // 4 + n * (4 * Q_BLOCK), Q_BLOCK, stride=4)
                obuf[br][rows, :] = ob
                lbuf[br][rows, :] = lb
                mbuf[br][rows, :] = mb

        def block_group(it, carry, one_block=one_block):
            for g in range(group):
                one_block(it * group + g)
            return carry
        lax.fori_loop(0, n_blocks // group, block_group, 0)

    for r4 in range(4):
        def finish(i, carry, r4=r4):
            rows = pl.ds(pl.multiple_of(r4 * quarter + i * chunk, chunk), chunk)
            ms = [mbuf[br][rows, :] for br in range(3)]
            m_all = jnp.maximum(jnp.maximum(ms[0], ms[1]), ms[2])
            num = den = None
            for br in range(3):
                w = jnp.exp(ms[br] - m_all)
                t_num = w * obuf[br][rows, :]
                t_den = w * lbuf[br][rows, :]
                num = t_num if num is None else num + t_num
                den = t_den if den is None else den + t_den
            o_ref[pl.ds(r4 + i * (4 * chunk), chunk, stride=4), :] = (num / den).astype(o_ref.dtype)
            return carry
        lax.fori_loop(0, quarter // chunk, finish, 0)


def _attention(proj3, cos2, sin2, gq, gk, bias, *, group=32):
    B, S, _ = proj3.shape
    head_spec = lambda off: pl.BlockSpec((None, S, HEAD_DIM), lambda b, h: (b, 0, off + h))
    const = lambda shape: pl.BlockSpec(shape, lambda b, h: (0,) * len(shape), pipeline_mode=pl.Buffered(1))
    return pl.pallas_call(
        functools.partial(_attn_kernel, seq=S, group=group),
        out_shape=jax.ShapeDtypeStruct((B, S, D_ATT), F32),
        grid=(B, N_HEADS),
        in_specs=[
            head_spec(0), head_spec(N_HEADS), head_spec(2 * N_HEADS),
            const((S, HEAD_DIM)), const((S, HEAD_DIM)),
            const((1, HEAD_DIM)), const((1, HEAD_DIM)),
            const((4, Q_BLOCK, 2 * Q_BLOCK)),
        ],
        out_specs=pl.BlockSpec((None, S, HEAD_DIM), lambda b, h: (b, 0, h)),
        scratch_shapes=[
            pltpu.VMEM((2, S, HEAD_DIM), BF16), pltpu.VMEM((3, S, HEAD_DIM), BF16),
            pltpu.VMEM((3, S, 2 * HEAD_DIM), BF16),
            pltpu.VMEM((S, HEAD_DIM), F32), pltpu.VMEM((S, HEAD_DIM), F32),
        ] + [pltpu.VMEM((S, HEAD_DIM), F32)] * 9,
        compiler_params=pltpu.CompilerParams(
            dimension_semantics=("arbitrary", "arbitrary"), vmem_limit_bytes=VMEM_LIMIT_BYTES),
        name="dilated_attention",
    )(proj3, proj3, proj3, cos2, sin2, gq, gk, bias)


CONV_HALO = 32


def _conv_kernel(a_ref, b_ref, gc_ref, dw_ref, dwb_ref, lng_ref, lnb_ref, wpw_ref, og_ref, o_ref,
                 u_ext, y_s, *, row_chunk):
    tm = a_ref.shape[0]

    @pl.when(pl.program_id(1) == 0)
    def _():
        u_ext[0:CONV_HALO, :] = jnp.zeros((CONV_HALO, D_CONV), F32)

    u_ext[CONV_HALO:CONV_HALO + tm, :] = a_ref[...].astype(F32) * jax.nn.sigmoid(b_ref[...].astype(F32))

    lead = CONV_HALO - (CONV_WIDTH - 1)

    def lanes_body(cc, carry):
        lanes = pl.ds(pl.multiple_of(cc * 128, 128), 128)
        for rc in range(tm // row_chunk):
            r0 = rc * row_chunk
            acc = jnp.broadcast_to(dwb_ref[:, lanes], (row_chunk, 128))
            for j in range(CONV_WIDTH):
                acc = acc + dw_ref[j:j + 1, lanes] * u_ext[r0 + lead + j:r0 + lead + j + row_chunk, lanes]
            y_s[r0:r0 + row_chunk, lanes] = acc
        return carry
    lax.fori_loop(0, D_CONV // 128, lanes_body, 0)

    u_ext[0:CONV_HALO, :] = u_ext[tm:tm + CONV_HALO, :]

    y = y_s[...]
    yc = y - jnp.mean(y, axis=-1, keepdims=True)
    yn = yc * lax.rsqrt(jnp.mean(yc * yc, axis=-1, keepdims=True) + EPS)
    z = _silu(yn * lng_ref[...] + lnb_ref[...]).astype(BF16)
    conv = jnp.dot(z, wpw_ref[...], preferred_element_type=F32)
    cn = conv * lax.rsqrt(jnp.mean(conv * conv, axis=-1, keepdims=True) + EPS) * og_ref[...]
    o_ref[...] = (cn * _silu(gc_ref[...].astype(F32))).astype(o_ref.dtype)


def _conv_module(proj3, dw, dwb, lng, lnb, wpw_bf, og, *, tm=256):
    B, S, _ = proj3.shape
    cb = D_CONV // D_CONV
    col = lambda blk: pl.BlockSpec((None, tm, D_CONV), lambda b, i: (b, i, blk))
    const2 = lambda shape: pl.BlockSpec(shape, lambda b, i: (0,) * len(shape))
    base = 4 * D_ATT // D_CONV
    return pl.pallas_call(
        functools.partial(_conv_kernel, row_chunk=64),
        out_shape=jax.ShapeDtypeStruct((B, S, D_CONV), BF16),
        grid=(B, S // tm),
        in_specs=[
            col(base), col(base + cb), col(base + 2 * cb),
            const2((CONV_WIDTH, D_CONV)), const2((1, D_CONV)), const2((1, D_CONV)), const2((1, D_CONV)),
            const2((D_CONV, D_CONV)), const2((1, D_CONV)),
        ],
        out_specs=pl.BlockSpec((None, tm, D_CONV), lambda b, i: (b, i, 0)),
        scratch_shapes=[pltpu.VMEM((tm + CONV_HALO, D_CONV), F32), pltpu.VMEM((tm, D_CONV), F32)],
        compiler_params=pltpu.CompilerParams(
            dimension_semantics=("arbitrary", "arbitrary"), vmem_limit_bytes=VMEM_LIMIT_BYTES),
        name="conformer_conv",
    )(proj3, proj3, proj3, dw, dwb, lng, lnb, wpw_bf, og)


def _out_kernel(att_ref, ga_ref, cy_ref, x_ref, g_ref, w_ref, o_ref, lhs, *, row_chunk):
    tm = att_ref.shape[0]

    @pl.when(pl.program_id(1) == 0)
    def _():
        def body(i, carry):
            rows = pl.ds(pl.multiple_of(i * row_chunk, row_chunk), row_chunk)
            a = att_ref[rows, :]
            an = a * lax.rsqrt(jnp.mean(a * a, axis=-1, keepdims=True) + EPS) * g_ref[...]
            lhs[rows, 0:D_ATT] = (an * _silu(ga_ref[rows, :].astype(F32))).astype(BF16)
            lhs[rows, D_ATT:D_MIX] = cy_ref[rows, :]
            return carry
        lax.fori_loop(0, tm // row_chunk, body, 0)

    o_ref[...] = x_ref[...] + jnp.dot(lhs[...], w_ref[...], preferred_element_type=F32)


def _out_proj(att2, proj2, convy2, x2, g, w_bf, *, tm=512, tn=512):
    T = x2.shape[0]
    gate_blk = 3 * D_ATT // D_ATT
    return pl.pallas_call(
        functools.partial(_out_kernel, row_chunk=128),
        out_shape=jax.ShapeDtypeStruct((T, D_MODEL), F32),
        grid=(T // tm, D_MODEL // tn),
        in_specs=[
            pl.BlockSpec((tm, D_ATT), lambda i, j: (i, 0)),
            pl.BlockSpec((tm, D_ATT), lambda i, j: (i, gate_blk)),
            pl.BlockSpec((tm, D_CONV), lambda i, j: (i, 0)),
            pl.BlockSpec((tm, tn), lambda i, j: (i, j)),
            pl.BlockSpec((1, D_ATT), lambda i, j: (0, 0)),
            pl.BlockSpec((D_MIX, tn), lambda i, j: (0, j)),
        ],
        out_specs=pl.BlockSpec((tm, tn), lambda i, j: (i, j)),
        scratch_shapes=[pltpu.VMEM((tm, D_MIX), BF16)],
        compiler_params=pltpu.CompilerParams(
            dimension_semantics=("arbitrary", "arbitrary"), vmem_limit_bytes=VMEM_LIMIT_BYTES),
        name="out_proj",
    )(att2, proj2, convy2, x2, g, w_bf)


def _rope_tables(seq):
    inv_freq = 1.0 / (ROPE_THETA ** (jnp.arange(0, HEAD_DIM, 2, dtype=F32) / HEAD_DIM))
    ang = jnp.arange(seq, dtype=F32)[:, None] * inv_freq[None, :]
    cos, sin = jnp.cos(ang), jnp.sin(ang)
    return jnp.concatenate([cos, cos], axis=-1), jnp.concatenate([-sin, sin], axis=-1)


def _mask_bias():
    rows = jnp.arange(Q_BLOCK)
    rows_mod4 = 4 * (rows % (Q_BLOCK // 4)) + rows // (Q_BLOCK // 4)
    ik = jnp.arange(2 * Q_BLOCK)[None, :]
    def bias(iq, shift):
        dist = iq[:, None] + shift - ik
        return jnp.where((dist >= 0) & (dist <= Q_BLOCK), 0.0, MASKED).astype(F32)
    return jnp.stack([bias(rows_mod4, 0), bias(rows_mod4, Q_BLOCK), bias(rows, 0), bias(rows, Q_BLOCK)])


def kernel(x, norm_g, w_in, q_norm_g, k_norm_g, dw_kernel, dw_bias, conv_ln_g, conv_ln_b, w_pw,
           att_out_g, conv_out_g, w_out):
    B, S, D = x.shape
    assert D == D_MODEL and S % (16 * 2 * Q_BLOCK) == 0
    depth = norm_g.shape[0]
    cos2, sin2 = _rope_tables(S)
    bias = _mask_bias()
    row = lambda v: v.reshape(1, -1)
    x2 = x.reshape(B * S, D)
    for layer in range(depth):
        proj2 = _norm_proj(x2, row(norm_g[layer]), w_in[layer].astype(BF16))
        proj3 = proj2.reshape(B, S, D_IN)
        att = _attention(proj3, cos2, sin2, row(q_norm_g[layer]), row(k_norm_g[layer]), bias)
        conv_y = _conv_module(proj3, dw_kernel[layer], row(dw_bias[layer]), row(conv_ln_g[layer]),
                              row(conv_ln_b[layer]), w_pw[layer].astype(BF16), row(conv_out_g[layer]))
        x2 = _out_proj(att.reshape(B * S, D_ATT), proj2, conv_y.reshape(B * S, D_CONV), x2,
                       row(att_out_g[layer]), w_out[layer].astype(BF16))
    return x2.reshape(B, S, D)
```

```python
import functools

import jax
import jax.numpy as jnp
from jax import lax
from jax.experimental import pallas as pl
from jax.experimental.pallas import tpu as pltpu

D_MODEL = 2048
N_HEADS = 16
HEAD_DIM = 128
D_ATT = N_HEADS * HEAD_DIM
D_CONV = D_MODEL
D_MIX = D_ATT + D_CONV
D_IN = 4 * D_ATT + 3 * D_CONV
CONV_WIDTH = 31
DILATIONS = (1, 4, 16)
Q_BLOCK = 128
ROPE_THETA = 10000.0
EPS = 1e-6
MASKED = -1e30

SUBLANES = 8

VMEM_LIMIT_BYTES = 56 * 1024 * 1024

F32 = jnp.float32
BF16 = jnp.bfloat16


def _silu(x):
    return x * jax.nn.sigmoid(x)


def _norm_proj_kernel(x_ref, g_ref, w_ref, o_ref, h_ref, *, row_chunk):
    tm = x_ref.shape[0]

    @pl.when(pl.program_id(1) == 0)
    def _():
        def body(i, carry):
            rows = pl.ds(pl.multiple_of(i * row_chunk, row_chunk), row_chunk)
            x = x_ref[rows, :]
            y = x * lax.rsqrt(jnp.mean(x * x, axis=-1, keepdims=True) + EPS)
            h_ref[rows, :] = (y * g_ref[...]).astype(BF16)
            return carry
        lax.fori_loop(0, tm // row_chunk, body, 0)

    o_ref[...] = jnp.dot(h_ref[...], w_ref[...], preferred_element_type=F32).astype(o_ref.dtype)


def _norm_proj(x2, g, w_bf, layer, *, tm=1024, tn=1024):
    T = x2.shape[0]
    return pl.pallas_call(
        functools.partial(_norm_proj_kernel, row_chunk=128),
        out_shape=jax.ShapeDtypeStruct((T, D_IN), BF16),
        grid=(T // tm, D_IN // tn),
        in_specs=[
            pl.BlockSpec((tm, D_MODEL), lambda i, j: (i, 0)),
            pl.BlockSpec((1, D_MODEL), lambda i, j: (0, 0)),
            pl.BlockSpec((None, D_MODEL, tn), lambda i, j: (layer, 0, j)),
        ],
        out_specs=pl.BlockSpec((tm, tn), lambda i, j: (i, j)),
        scratch_shapes=[pltpu.VMEM((tm, D_MODEL), BF16)],
        compiler_params=pltpu.CompilerParams(
            dimension_semantics=("arbitrary", "arbitrary"), vmem_limit_bytes=VMEM_LIMIT_BYTES),
        name="norm_proj",
    )(x2, g, w_bf)


NAT, MOD4, MOD16 = 0, 1, 2


def _attn_kernel(q_ref, k_ref, v_ref, cos_ref, sin_ref, gq_ref, gk_ref, bias_ref, o_ref,
                 qs, ks, vs, t0, t1, *stats, seq, group):
    obuf, lbuf, mbuf = stats[0:3], stats[3:6], stats[6:9]
    chunk = 256
    quarter = seq // 4
    sixteenth = seq // 16

    @pl.when((pl.program_id(0) == 0) & (pl.program_id(1) == 0))
    def _():
        for layout in range(3):
            vs[layout, :, HEAD_DIM:2 * HEAD_DIM] = jnp.ones((seq, HEAD_DIM), BF16)

    def prepare(src_ref, g_ref, dst, first_layout, normalize, scale):
        def body(i, carry):
            rows = pl.ds(pl.multiple_of(i * chunk, chunk), chunk)
            y = src_ref[rows, :].astype(F32)
            if normalize:
                y = y * lax.rsqrt(jnp.mean(y * y, axis=-1, keepdims=True) + EPS) * g_ref[...]
                y = y * cos_ref[rows, :] + pltpu.roll(y, HEAD_DIM // 2, 1) * sin_ref[rows, :]
                if scale != 1.0:
                    y = y * scale
            t0[rows, :] = y
            if first_layout == NAT:
                dst[NAT, rows, 0:HEAD_DIM] = y.astype(BF16)
            return carry
        lax.fori_loop(0, seq // chunk, body, 0, unroll=4)

        for r4 in range(4):
            def body4(i, carry, r4=r4):
                y = t0[pl.ds(r4 + i * (4 * chunk), chunk, stride=4), :]
                rows = pl.ds(pl.multiple_of(r4 * quarter + i * chunk, chunk), chunk)
                t1[rows, :] = y
                dst[MOD4 - first_layout, rows, 0:HEAD_DIM] = y.astype(BF16)
                return carry
            lax.fori_loop(0, quarter // chunk, body4, 0)

        for rp in range(4):
            for r4 in range(4):
                y = t1[pl.ds(r4 * quarter + rp, sixteenth, stride=4), :]
                dst[MOD16 - first_layout, pl.ds((4 * rp + r4) * sixteenth, sixteenth), 0:HEAD_DIM] = y.astype(BF16)

    prepare(q_ref, gq_ref, qs, MOD4, True, HEAD_DIM ** -0.5)
    prepare(k_ref, gk_ref, ks, NAT, True, 1.0)
    prepare(v_ref, None, vs, NAT, False, 1.0)

    n_blocks = seq // Q_BLOCK
    piece = Q_BLOCK // 4
    for br, d in enumerate(DILATIONS):
        nb = n_blocks // d

        def one_block(c, br=br, d=d, nb=nb):
            n = lax.rem(c, nb)
            first = n == 0
            q_row = pl.multiple_of(c * Q_BLOCK, Q_BLOCK)
            k_row = pl.multiple_of(jnp.where(first, c * Q_BLOCK, (c - 1) * Q_BLOCK), Q_BLOCK)
            if d == 1:
                runs = [pl.ds(pl.multiple_of(r4 * quarter + c * piece, piece), piece) for r4 in range(4)]
                qb = jnp.concatenate([qs[0, run, :] for run in runs], axis=0)
            else:
                qb = qs[br - 1, pl.ds(q_row, Q_BLOCK), :]
            kb = ks[br, pl.ds(k_row, 2 * Q_BLOCK), :]
            vb = vs[br, pl.ds(k_row, 2 * Q_BLOCK), :]
            s = lax.dot_general(qb, kb, (((1,), (1,)), ((), ())), preferred_element_type=F32)
            s = s + bias_ref[jnp.where(first, 0, 1) + (0 if d == 1 else 2)]
            m = jnp.max(s, axis=-1, keepdims=True)
            p = jnp.exp(s - m).astype(BF16)
            oe = jnp.dot(p, vb, preferred_element_type=F32)
            mb = jnp.broadcast_to(m, (Q_BLOCK, HEAD_DIM))
            ob, lb = oe[:, 0:HEAD_DIM], oe[:, HEAD_DIM:2 * HEAD_DIM]
            if d == 1:
                for r4, run in enumerate(runs):
                    sub = slice(r4 * piece, (r4 + 1) * piece)
                    obuf[br][run, :] = ob[sub, :]
                    lbuf[br][run, :] = lb[sub, :]
                    mbuf[br][run, :] = mb[sub, :]
            else:
                if d == 4:
                    rows = pl.ds(q_row, Q_BLOCK)
                else:
                    r = c // nb
                    rows = pl.ds((r % 4) * quarter + r // 4 + n * (4 * Q_BLOCK), Q_BLOCK, stride=4)
                obuf[br][rows, :] = ob
                lbuf[br][rows, :] = lb
                mbuf[br][rows, :] = mb

        def block_group(it, carry, one_block=one_block):
            for g in range(group):
                one_block(it * group + g)
            return carry
        lax.fori_loop(0, n_blocks // group, block_group, 0)

    for r4 in range(4):
        def finish(i, carry, r4=r4):
            rows = pl.ds(pl.multiple_of(r4 * quarter + i * chunk, chunk), chunk)
            ms = [mbuf[br][rows, :] for br in range(3)]
            m_all = jnp.maximum(jnp.maximum(ms[0], ms[1]), ms[2])
            num = den = None
            for br in range(3):
                w = jnp.exp(ms[br] - m_all)
                t_num = w * obuf[br][rows, :]
                t_den = w * lbuf[br][rows, :]
                num = t_num if num is None else num + t_num
                den = t_den if den is None else den + t_den
            o_ref[pl.ds(r4 + i * (4 * chunk), chunk, stride=4), :] = (num / den).astype(o_ref.dtype)
            return carry
        lax.fori_loop(0, quarter // chunk, finish, 0)


def _attention(proj3, cos2, sin2, gq, gk, bias, *, group=32):
    B, S, _ = proj3.shape
    head_spec = lambda off: pl.BlockSpec((None, S, HEAD_DIM), lambda b, h: (b, 0, off + h))
    const = lambda shape: pl.BlockSpec(shape, lambda b, h: (0,) * len(shape), pipeline_mode=pl.Buffered(1))
    return pl.pallas_call(
        functools.partial(_attn_kernel, seq=S, group=group),
        out_shape=jax.ShapeDtypeStruct((B, S, D_ATT), F32),
        grid=(B, N_HEADS),
        in_specs=[
            head_spec(0), head_spec(N_HEADS), head_spec(2 * N_HEADS),
            const((S, HEAD_DIM)), const((S, HEAD_DIM)),
            const((1, HEAD_DIM)), const((1, HEAD_DIM)),
            const((4, Q_BLOCK, 2 * Q_BLOCK)),
        ],
        out_specs=pl.BlockSpec((None, S, HEAD_DIM), lambda b, h: (b, 0, h)),
        scratch_shapes=[
            pltpu.VMEM((2, S, HEAD_DIM), BF16), pltpu.VMEM((3, S, HEAD_DIM), BF16),
            pltpu.VMEM((3, S, 2 * HEAD_DIM), BF16),
            pltpu.VMEM((S, HEAD_DIM), F32), pltpu.VMEM((S, HEAD_DIM), F32),
        ] + [pltpu.VMEM((S, HEAD_DIM), F32)] * 9,
        compiler_params=pltpu.CompilerParams(
            dimension_semantics=("arbitrary", "arbitrary"), vmem_limit_bytes=VMEM_LIMIT_BYTES),
        name="dilated_attention",
    )(proj3, proj3, proj3, cos2, sin2, gq, gk, bias)


CONV_HALO = 32


def _conv_kernel(a_ref, b_ref, gc_ref, dw_ref, dwb_ref, lng_ref, lnb_ref, wpw_ref, og_ref, o_ref,
                 u_ext, y_s, z_s, *, row_chunk, conv_rows):
    tm = a_ref.shape[0]

    def row_loop(body, n_rows):
        def step(i, carry):
            body(pl.ds(pl.multiple_of(i * n_rows, n_rows), n_rows),
                 pl.ds(pl.multiple_of(i * n_rows + CONV_HALO, SUBLANES), n_rows))
            return carry
        lax.fori_loop(0, tm // n_rows, step, 0)

    @pl.when(pl.program_id(1) == 0)
    def _():
        u_ext[0:CONV_HALO, :] = jnp.zeros((CONV_HALO, D_CONV), F32)

    def glu(rows, ext_rows):
        u_ext[ext_rows, :] = a_ref[rows, :].astype(F32) * jax.nn.sigmoid(b_ref[rows, :].astype(F32))
    row_loop(glu, 2 * row_chunk)

    lead = CONV_HALO - (CONV_WIDTH - 1)

    def lanes_body(cc, carry):
        lanes = pl.ds(pl.multiple_of(cc * 128, 128), 128)
        for rc in range(tm // conv_rows):
            r0 = rc * conv_rows
            acc = jnp.broadcast_to(dwb_ref[:, lanes], (conv_rows, 128))
            for shift in range(SUBLANES):
                n_rows = conv_rows + (SUBLANES if shift else 0)
                part = None
                for off in range(shift, lead + CONV_WIDTH, SUBLANES):
                    if off < lead:
                        continue
                    start = r0 + off - shift
                    term = dw_ref[off - lead:off - lead + 1, lanes] * u_ext[start:start + n_rows, lanes]
                    part = term if part is None else part + term
                acc = acc + part[shift:shift + conv_rows, :]
            y_s[r0:r0 + conv_rows, lanes] = acc
        return carry
    lax.fori_loop(0, D_CONV // 128, lanes_body, 0)

    u_ext[0:CONV_HALO, :] = u_ext[tm:tm + CONV_HALO, :]

    def layer_norm_swish(rows, ext_rows):
        y = y_s[rows, :]
        yc = y - jnp.mean(y, axis=-1, keepdims=True)
        yn = yc * lax.rsqrt(jnp.mean(yc * yc, axis=-1, keepdims=True) + EPS)
        z_s[rows, :] = _silu(yn * lng_ref[...] + lnb_ref[...]).astype(BF16)
    row_loop(layer_norm_swish, 2 * row_chunk)

    y_s[...] = jnp.dot(z_s[...], wpw_ref[...], preferred_element_type=F32)

    def norm_gate(rows, ext_rows):
        conv = y_s[rows, :]
        cn = conv * lax.rsqrt(jnp.mean(conv * conv, axis=-1, keepdims=True) + EPS) * og_ref[...]
        o_ref[rows, :] = (cn * _silu(gc_ref[rows, :].astype(F32))).astype(o_ref.dtype)
    row_loop(norm_gate, row_chunk)


def _conv_module(proj3, dw, dwb, lng, lnb, wpw_bf, og, layer, *, tm=512):
    B, S, _ = proj3.shape
    cb = D_CONV // D_CONV
    col = lambda blk: pl.BlockSpec((None, tm, D_CONV), lambda b, i: (b, i, blk))
    const = lambda shape: pl.BlockSpec(shape, lambda b, i: (0,) * len(shape), pipeline_mode=pl.Buffered(1))
    base = 4 * D_ATT // D_CONV
    return pl.pallas_call(
        functools.partial(_conv_kernel, row_chunk=64, conv_rows=128),
        out_shape=jax.ShapeDtypeStruct((B, S, D_CONV), BF16),
        grid=(B, S // tm),
        in_specs=[
            col(base), col(base + cb), col(base + 2 * cb),
            const((CONV_WIDTH, D_CONV)), const((1, D_CONV)), const((1, D_CONV)), const((1, D_CONV)),
            pl.BlockSpec((None, D_CONV, D_CONV), lambda b, i: (layer, 0, 0), pipeline_mode=pl.Buffered(1)),
            const((1, D_CONV)),
        ],
        out_specs=pl.BlockSpec((None, tm, D_CONV), lambda b, i: (b, i, 0)),
        scratch_shapes=[pltpu.VMEM((tm + CONV_HALO, D_CONV), F32), pltpu.VMEM((tm, D_CONV), F32),
                        pltpu.VMEM((tm, D_CONV), BF16)],
        compiler_params=pltpu.CompilerParams(
            dimension_semantics=("arbitrary", "arbitrary"), vmem_limit_bytes=VMEM_LIMIT_BYTES),
        name="conformer_conv",
    )(proj3, proj3, proj3, dw, dwb, lng, lnb, wpw_bf, og)


def _out_kernel(att_ref, ga_ref, cy_ref, x_ref, g_ref, w_ref, o_ref, lhs, *, row_chunk):
    tm = att_ref.shape[0]

    def body(i, carry):
        rows = pl.ds(pl.multiple_of(i * row_chunk, row_chunk), row_chunk)
        a = att_ref[rows, :]
        an = a * lax.rsqrt(jnp.mean(a * a, axis=-1, keepdims=True) + EPS) * g_ref[...]
        lhs[rows, 0:D_ATT] = (an * _silu(ga_ref[rows, :].astype(F32))).astype(BF16)
        lhs[rows, D_ATT:D_MIX] = cy_ref[rows, :]
        return carry
    lax.fori_loop(0, tm // row_chunk, body, 0, unroll=2)

    o_ref[...] = x_ref[...] + jnp.dot(lhs[...], w_ref[...], preferred_element_type=F32)


def _out_proj(att2, proj2, convy2, x2, g, w_bf, layer, *, tm=256):
    T = x2.shape[0]
    gate_blk = 3 * D_ATT // D_ATT
    return pl.pallas_call(
        functools.partial(_out_kernel, row_chunk=16),
        out_shape=jax.ShapeDtypeStruct((T, D_MODEL), F32),
        grid=(T // tm,),
        in_specs=[
            pl.BlockSpec((tm, D_ATT), lambda i: (i, 0)),
            pl.BlockSpec((tm, D_ATT), lambda i: (i, gate_blk)),
            pl.BlockSpec((tm, D_CONV), lambda i: (i, 0)),
            pl.BlockSpec((tm, D_MODEL), lambda i: (i, 0)),
            pl.BlockSpec((1, D_ATT), lambda i: (0, 0), pipeline_mode=pl.Buffered(1)),
            pl.BlockSpec((None, D_MIX, D_MODEL), lambda i: (layer, 0, 0), pipeline_mode=pl.Buffered(1)),
        ],
        out_specs=pl.BlockSpec((tm, D_MODEL), lambda i: (i, 0)),
        scratch_shapes=[pltpu.VMEM((tm, D_MIX), BF16)],
        compiler_params=pltpu.CompilerParams(
            dimension_semantics=("arbitrary",), vmem_limit_bytes=VMEM_LIMIT_BYTES),
        name="out_proj",
    )(att2, proj2, convy2, x2, g, w_bf)


def _rope_tables(seq):
    inv_freq = 1.0 / (ROPE_THETA ** (jnp.arange(0, HEAD_DIM, 2, dtype=F32) / HEAD_DIM))
    ang = jnp.arange(seq, dtype=F32)[:, None] * inv_freq[None, :]
    cos, sin = jnp.cos(ang), jnp.sin(ang)
    return jnp.concatenate([cos, cos], axis=-1), jnp.concatenate([-sin, sin], axis=-1)


def _mask_bias():
    rows = jnp.arange(Q_BLOCK)
    rows_mod4 = 4 * (rows % (Q_BLOCK // 4)) + rows // (Q_BLOCK // 4)
    ik = jnp.arange(2 * Q_BLOCK)[None, :]
    def bias(iq, shift):
        dist = iq[:, None] + shift - ik
        return jnp.where((dist >= 0) & (dist <= Q_BLOCK), 0.0, MASKED).astype(F32)
    return jnp.stack([bias(rows_mod4, 0), bias(rows_mod4, Q_BLOCK), bias(rows, 0), bias(rows, Q_BLOCK)])


def kernel(x, norm_g, w_in, q_norm_g, k_norm_g, dw_kernel, dw_bias, conv_ln_g, conv_ln_b, w_pw,
           att_out_g, conv_out_g, w_out):
    B, S, D = x.shape
    assert D == D_MODEL and S % (16 * 2 * Q_BLOCK) == 0
    depth = norm_g.shape[0]
    cos2, sin2 = _rope_tables(S)
    bias = _mask_bias()
    row = lambda v: v.reshape(1, -1)
    x2 = x.reshape(B * S, D)
    w_in_bf, w_pw_bf, w_out_bf = w_in.astype(BF16), w_pw.astype(BF16), w_out.astype(BF16)
    for layer in range(depth):
        proj2 = _norm_proj(x2, row(norm_g[layer]), w_in_bf, layer)
        proj3 = proj2.reshape(B, S, D_IN)
        att = _attention(proj3, cos2, sin2, row(q_norm_g[layer]), row(k_norm_g[layer]), bias)
        conv_y = _conv_module(proj3, dw_kernel[layer], row(dw_bias[layer]), row(conv_ln_g[layer]),
                              row(conv_ln_b[layer]), w_pw_bf, row(conv_out_g[layer]), layer)
        x2 = _out_proj(att.reshape(B * S, D_ATT), proj2, conv_y.reshape(B * S, D_CONV), x2,
                       row(att_out_g[layer]), w_out_bf, layer)
    return x2.reshape(B, S, D)
```

```python
import functools

import jax
import jax.numpy as jnp
from jax import lax
from jax.experimental import pallas as pl
from jax.experimental.pallas import tpu as pltpu

D_MODEL = 2048
N_HEADS = 16
HEAD_DIM = 128
D_ATT = N_HEADS * HEAD_DIM
D_CONV = D_MODEL
D_MIX = D_ATT + D_CONV
D_IN = 4 * D_ATT + 3 * D_CONV
CONV_WIDTH = 31
DILATIONS = (1, 4, 16)
Q_BLOCK = 128
ROPE_THETA = 10000.0
EPS = 1e-6
MASKED = -1e30

SUBLANES = 8

VMEM_LIMIT_BYTES = 56 * 1024 * 1024

F32 = jnp.float32
BF16 = jnp.bfloat16


def _sigmoid(x):
    return 0.5 * jnp.tanh(0.5 * x) + 0.5


def _silu(x):
    h = 0.5 * x
    return h * jnp.tanh(h) + h


def _norm_proj_kernel(x_ref, g_ref, w_ref, o_ref, h_ref, *, row_chunk):
    tm = x_ref.shape[0]

    @pl.when(pl.program_id(1) == 0)
    def _():
        def body(i, carry):
            rows = pl.ds(pl.multiple_of(i * row_chunk, row_chunk), row_chunk)
            x = x_ref[rows, :]
            y = x * lax.rsqrt(jnp.mean(x * x, axis=-1, keepdims=True) + EPS)
            h_ref[rows, :] = (y * g_ref[...]).astype(BF16)
            return carry
        lax.fori_loop(0, tm // row_chunk, body, 0)

    o_ref[...] = jnp.dot(h_ref[...], w_ref[...], preferred_element_type=F32).astype(o_ref.dtype)


def _norm_proj(x2, g, w_bf, layer, *, tm=1024, tn=1024):
    T = x2.shape[0]
    return pl.pallas_call(
        functools.partial(_norm_proj_kernel, row_chunk=128),
        out_shape=jax.ShapeDtypeStruct((T, D_IN), BF16),
        grid=(T // tm, D_IN // tn),
        in_specs=[
            pl.BlockSpec((tm, D_MODEL), lambda i, j: (i, 0)),
            pl.BlockSpec((1, D_MODEL), lambda i, j: (0, 0)),
            pl.BlockSpec((None, D_MODEL, tn), lambda i, j: (layer, 0, j)),
        ],
        out_specs=pl.BlockSpec((tm, tn), lambda i, j: (i, j)),
        scratch_shapes=[pltpu.VMEM((tm, D_MODEL), BF16)],
        compiler_params=pltpu.CompilerParams(
            dimension_semantics=("arbitrary", "arbitrary"), vmem_limit_bytes=VMEM_LIMIT_BYTES),
        name="norm_proj",
    )(x2, g, w_bf)


NAT, MOD4, MOD16 = 0, 1, 2


def _attn_kernel(q_ref, k_ref, v_ref, cos_ref, sin_ref, gq_ref, gk_ref, bias_ref, o_ref,
                 qs, ks, vs, t0, t1, *stats, seq, group):
    obuf, lbuf, mbuf = stats[0:3], stats[3:6], stats[6:9]
    chunk = 256
    quarter = seq // 4
    sixteenth = seq // 16

    @pl.when((pl.program_id(0) == 0) & (pl.program_id(1) == 0))
    def _():
        for layout in range(3):
            vs[layout, :, HEAD_DIM:2 * HEAD_DIM] = jnp.ones((seq, HEAD_DIM), BF16)

    def prepare(src_ref, g_ref, dst, first_layout, normalize, scale):
        if normalize:
            gain = g_ref[...] * scale
            gain_rot = pltpu.roll(gain, HEAD_DIM // 2, 1)

        def body(i, carry):
            rows = pl.ds(pl.multiple_of(i * chunk, chunk), chunk)
            raw = src_ref[rows, :]
            y = raw.astype(F32)
            if normalize:
                rot = pltpu.roll(raw, HEAD_DIM // 2, 1).astype(F32)
                inv = lax.rsqrt(jnp.mean(y * y, axis=-1, keepdims=True) + EPS)
                y = (y * gain * cos_ref[rows, :] + rot * gain_rot * sin_ref[rows, :]) * inv
            t0[rows, :] = y
            if first_layout == NAT:
                dst[NAT, rows, 0:HEAD_DIM] = y.astype(BF16)
            return carry
        lax.fori_loop(0, seq // chunk, body, 0, unroll=4)

        for r4 in range(4):
            def body4(i, carry, r4=r4):
                y = t0[pl.ds(r4 + i * (4 * chunk), chunk, stride=4), :]
                rows = pl.ds(pl.multiple_of(r4 * quarter + i * chunk, chunk), chunk)
                t1[rows, :] = y
                dst[MOD4 - first_layout, rows, 0:HEAD_DIM] = y.astype(BF16)
                return carry
            lax.fori_loop(0, quarter // chunk, body4, 0)

        for rp in range(4):
            for r4 in range(4):
                y = t1[pl.ds(r4 * quarter + rp, sixteenth, stride=4), :]
                dst[MOD16 - first_layout, pl.ds((4 * rp + r4) * sixteenth, sixteenth), 0:HEAD_DIM] = y.astype(BF16)

    prepare(q_ref, gq_ref, qs, MOD4, True, HEAD_DIM ** -0.5)
    prepare(k_ref, gk_ref, ks, NAT, True, 1.0)
    prepare(v_ref, None, vs, NAT, False, 1.0)

    n_blocks = seq // Q_BLOCK
    piece = Q_BLOCK // 4
    for br, d in enumerate(DILATIONS):
        nb = n_blocks // d

        def one_block(c, br=br, d=d, nb=nb):
            n = lax.rem(c, nb)
            first = n == 0
            q_row = pl.multiple_of(c * Q_BLOCK, Q_BLOCK)
            k_row = pl.multiple_of(jnp.where(first, c * Q_BLOCK, (c - 1) * Q_BLOCK), Q_BLOCK)
            if d == 1:
                runs = [pl.ds(pl.multiple_of(r4 * quarter + c * piece, piece), piece) for r4 in range(4)]
                qb = jnp.concatenate([qs[0, run, :] for run in runs], axis=0)
            else:
                qb = qs[br - 1, pl.ds(q_row, Q_BLOCK), :]
            kb = ks[br, pl.ds(k_row, 2 * Q_BLOCK), :]
            vb = vs[br, pl.ds(k_row, 2 * Q_BLOCK), :]
            s = lax.dot_general(qb, kb, (((1,), (1,)), ((), ())), preferred_element_type=F32)
            s = s + bias_ref[jnp.where(first, 0, 1) + (0 if d == 1 else 2)]
            m = jnp.max(s, axis=-1, keepdims=True)
            p = jnp.exp(s - m).astype(BF16)
            oe = jnp.dot(p, vb, preferred_element_type=F32)
            mb = jnp.broadcast_to(m, (Q_BLOCK, HEAD_DIM))
            ob, lb = oe[:, 0:HEAD_DIM], oe[:, HEAD_DIM:2 * HEAD_DIM]
            if d == 1:
                for r4, run in enumerate(runs):
                    sub = slice(r4 * piece, (r4 + 1) * piece)
                    obuf[br][run, :] = ob[sub, :]
                    lbuf[br][run, :] = lb[sub, :]
                    mbuf[br][run, :] = mb[sub, :]
            else:
                if d == 4:
                    rows = pl.ds(q_row, Q_BLOCK)
                else:
                    r = c // nb
                    rows = pl.ds((r % 4) * quarter + r // 4 + n * (4 * Q_BLOCK), Q_BLOCK, stride=4)
                obuf[br][rows, :] = ob
                lbuf[br][rows, :] = lb
                mbuf[br][rows, :] = mb

        def block_group(it, carry, one_block=one_block):
            for g in range(group):
                one_block(it * group + g)
            return carry
        lax.fori_loop(0, n_blocks // group, block_group, 0)

    for r4 in range(4):
        def finish(i, carry, r4=r4):
            rows = pl.ds(pl.multiple_of(r4 * quarter + i * chunk, chunk), chunk)
            ms = [mbuf[br][rows, :] for br in range(3)]
            m_all = jnp.maximum(jnp.maximum(ms[0], ms[1]), ms[2])
            num = den = None
            for br in range(3):
                w = jnp.exp(ms[br] - m_all)
                t_num = w * obuf[br][rows, :]
                t_den = w * lbuf[br][rows, :]
                num = t_num if num is None else num + t_num
                den = t_den if den is None else den + t_den
            o_ref[pl.ds(r4 + i * (4 * chunk), chunk, stride=4), :] = (num / den).astype(o_ref.dtype)
            return carry
        lax.fori_loop(0, quarter // chunk, finish, 0)


def _attention(proj3, cos2, sin2, gq, gk, bias, *, group=32):
    B, S, _ = proj3.shape
    head_spec = lambda off: pl.BlockSpec((None, S, HEAD_DIM), lambda b, h: (b, 0, off + h))
    const = lambda shape: pl.BlockSpec(shape, lambda b, h: (0,) * len(shape), pipeline_mode=pl.Buffered(1))
    return pl.pallas_call(
        functools.partial(_attn_kernel, seq=S, group=group),
        out_shape=jax.ShapeDtypeStruct((B, S, D_ATT), F32),
        grid=(B, N_HEADS),
        in_specs=[
            head_spec(0), head_spec(N_HEADS), head_spec(2 * N_HEADS),
            const((S, HEAD_DIM)), const((S, HEAD_DIM)),
            const((1, HEAD_DIM)), const((1, HEAD_DIM)),
            const((4, Q_BLOCK, 2 * Q_BLOCK)),
        ],
        out_specs=pl.BlockSpec((None, S, HEAD_DIM), lambda b, h: (b, 0, h)),
        scratch_shapes=[
            pltpu.VMEM((2, S, HEAD_DIM), BF16), pltpu.VMEM((3, S, HEAD_DIM), BF16),
            pltpu.VMEM((3, S, 2 * HEAD_DIM), BF16),
            pltpu.VMEM((S, HEAD_DIM), F32), pltpu.VMEM((S, HEAD_DIM), F32),
        ] + [pltpu.VMEM((S, HEAD_DIM), F32)] * 9,
        compiler_params=pltpu.CompilerParams(
            dimension_semantics=("arbitrary", "arbitrary"), vmem_limit_bytes=VMEM_LIMIT_BYTES),
        name="dilated_attention",
    )(proj3, proj3, proj3, cos2, sin2, gq, gk, bias)


CONV_HALO = 32


def _conv_kernel(a_ref, b_ref, gc_ref, dw_ref, dwb_ref, lng_ref, lnb_ref, wpw_ref, og_ref, o_ref,
                 u_ext, y_s, z_s, *, row_chunk, conv_rows):
    tm = a_ref.shape[0]

    def row_loop(body, n_rows):
        def step(i, carry):
            body(pl.ds(pl.multiple_of(i * n_rows, n_rows), n_rows),
                 pl.ds(pl.multiple_of(i * n_rows + CONV_HALO, SUBLANES), n_rows))
            return carry
        lax.fori_loop(0, tm // n_rows, step, 0)

    @pl.when(pl.program_id(1) == 0)
    def _():
        u_ext[0:CONV_HALO, :] = jnp.zeros((CONV_HALO, D_CONV), F32)

    def glu(rows, ext_rows):
        u_ext[ext_rows, :] = a_ref[rows, :].astype(F32) * _sigmoid(b_ref[rows, :].astype(F32))
    row_loop(glu, 2 * row_chunk)

    lead = CONV_HALO - (CONV_WIDTH - 1)

    def lanes_body(cc, carry):
        lanes = pl.ds(pl.multiple_of(cc * 128, 128), 128)
        for rc in range(tm // conv_rows):
            r0 = rc * conv_rows
            acc = jnp.broadcast_to(dwb_ref[:, lanes], (conv_rows, 128))
            for shift in range(SUBLANES):
                n_rows = conv_rows + (SUBLANES if shift else 0)
                part = None
                for off in range(shift, lead + CONV_WIDTH, SUBLANES):
                    if off < lead:
                        continue
                    start = r0 + off - shift
                    term = dw_ref[off - lead:off - lead + 1, lanes] * u_ext[start:start + n_rows, lanes]
                    part = term if part is None else part + term
                acc = acc + part[shift:shift + conv_rows, :]
            y_s[r0:r0 + conv_rows, lanes] = acc
        return carry
    lax.fori_loop(0, D_CONV // 128, lanes_body, 0)

    u_ext[0:CONV_HALO, :] = u_ext[tm:tm + CONV_HALO, :]

    def layer_norm_swish(rows, ext_rows):
        y = y_s[rows, :]
        yc = y - jnp.mean(y, axis=-1, keepdims=True)
        yn = yc * lax.rsqrt(jnp.mean(yc * yc, axis=-1, keepdims=True) + EPS)
        z_s[rows, :] = _silu(yn * lng_ref[...] + lnb_ref[...]).astype(BF16)
    row_loop(layer_norm_swish, 2 * row_chunk)

    y_s[...] = jnp.dot(z_s[...], wpw_ref[...], preferred_element_type=F32)

    def norm_gate(rows, ext_rows):
        conv = y_s[rows, :]
        cn = conv * lax.rsqrt(jnp.mean(conv * conv, axis=-1, keepdims=True) + EPS) * og_ref[...]
        o_ref[rows, :] = (cn * _silu(gc_ref[rows, :].astype(F32))).astype(o_ref.dtype)
    row_loop(norm_gate, row_chunk)


def _conv_module(proj3, dw, dwb, lng, lnb, wpw_bf, og, layer, *, tm=512):
    B, S, _ = proj3.shape
    cb = D_CONV // D_CONV
    col = lambda blk: pl.BlockSpec((None, tm, D_CONV), lambda b, i: (b, i, blk))
    const = lambda shape: pl.BlockSpec(shape, lambda b, i: (0,) * len(shape), pipeline_mode=pl.Buffered(1))
    base = 4 * D_ATT // D_CONV
    return pl.pallas_call(
        functools.partial(_conv_kernel, row_chunk=64, conv_rows=128),
        out_shape=jax.ShapeDtypeStruct((B, S, D_CONV), BF16),
        grid=(B, S // tm),
        in_specs=[
            col(base), col(base + cb), col(base + 2 * cb),
            const((CONV_WIDTH, D_CONV)), const((1, D_CONV)), const((1, D_CONV)), const((1, D_CONV)),
            pl.BlockSpec((None, D_CONV, D_CONV), lambda b, i: (layer, 0, 0), pipeline_mode=pl.Buffered(1)),
            const((1, D_CONV)),
        ],
        out_specs=pl.BlockSpec((None, tm, D_CONV), lambda b, i: (b, i, 0)),
        scratch_shapes=[pltpu.VMEM((tm + CONV_HALO, D_CONV), F32), pltpu.VMEM((tm, D_CONV), F32),
                        pltpu.VMEM((tm, D_CONV), BF16)],
        compiler_params=pltpu.CompilerParams(
            dimension_semantics=("arbitrary", "arbitrary"), vmem_limit_bytes=VMEM_LIMIT_BYTES),
        name="conformer_conv",
    )(proj3, proj3, proj3, dw, dwb, lng, lnb, wpw_bf, og)


def _out_kernel(att_ref, ga_ref, cy_ref, x_ref, g_ref, w_ref, o_ref, lhs_a, lhs_b, *, row_chunk, n_tiles):
    s = pl.program_id(0)
    tm = att_ref.shape[0]

    def gate_rows(dst, rows):
        a = att_ref[rows, :]
        an = a * lax.rsqrt(jnp.mean(a * a, axis=-1, keepdims=True) + EPS) * g_ref[...]
        dst[rows, 0:D_ATT] = (an * _silu(ga_ref[rows, :].astype(F32))).astype(BF16)
        dst[rows, D_ATT:D_MIX] = cy_ref[rows, :]

    def project(src):
        o_ref[...] = x_ref[...] + jnp.dot(src[...], w_ref[...], preferred_element_type=F32)

    @pl.when(s == 0)
    def _():
        def body(i, carry):
            gate_rows(lhs_a, pl.ds(pl.multiple_of(i * row_chunk, row_chunk), row_chunk))
            return carry
        lax.fori_loop(0, tm // row_chunk, body, 0)

    def steady(src, dst):
        project(src)
        for i in range(tm // row_chunk):
            gate_rows(dst, pl.ds(i * row_chunk, row_chunk))

    @pl.when((s > 0) & (s < n_tiles) & (lax.rem(s, 2) == 1))
    def _():
        steady(lhs_a, lhs_b)

    @pl.when((s > 0) & (s < n_tiles) & (lax.rem(s, 2) == 0))
    def _():
        steady(lhs_b, lhs_a)

    @pl.when(s == n_tiles)
    def _():
        project(lhs_a if n_tiles % 2 == 1 else lhs_b)


def _out_proj(att2, proj2, convy2, x2, g, w_bf, layer, *, tm=256):
    T = x2.shape[0]
    n_tiles = T // tm
    gate_blk = 3 * D_ATT // D_ATT
    ahead = lambda blk: (lambda s: (jnp.minimum(s, n_tiles - 1), blk))
    behind = lambda s: (jnp.maximum(s - 1, 0), 0)
    return pl.pallas_call(
        functools.partial(_out_kernel, row_chunk=16, n_tiles=n_tiles),
        out_shape=jax.ShapeDtypeStruct((T, D_MODEL), F32),
        grid=(n_tiles + 1,),
        in_specs=[
            pl.BlockSpec((tm, D_ATT), ahead(0)),
            pl.BlockSpec((tm, D_ATT), ahead(gate_blk)),
            pl.BlockSpec((tm, D_CONV), ahead(0)),
            pl.BlockSpec((tm, D_MODEL), behind),
            pl.BlockSpec((1, D_ATT), lambda s: (0, 0), pipeline_mode=pl.Buffered(1)),
            pl.BlockSpec((None, D_MIX, D_MODEL), lambda s: (layer, 0, 0), pipeline_mode=pl.Buffered(1)),
        ],
        out_specs=pl.BlockSpec((tm, D_MODEL), behind),
        scratch_shapes=[pltpu.VMEM((tm, D_MIX), BF16), pltpu.VMEM((tm, D_MIX), BF16)],
        compiler_params=pltpu.CompilerParams(
            dimension_semantics=("arbitrary",), vmem_limit_bytes=VMEM_LIMIT_BYTES),
        name="out_proj",
    )(att2, proj2, convy2, x2, g, w_bf)


def _rope_tables(seq):
    inv_freq = 1.0 / (ROPE_THETA ** (jnp.arange(0, HEAD_DIM, 2, dtype=F32) / HEAD_DIM))
    ang = jnp.arange(seq, dtype=F32)[:, None] * inv_freq[None, :]
    cos, sin = jnp.cos(ang), jnp.sin(ang)
    return jnp.concatenate([cos, cos], axis=-1), jnp.concatenate([-sin, sin], axis=-1)


def _mask_bias():
    rows = jnp.arange(Q_BLOCK)
    rows_mod4 = 4 * (rows % (Q_BLOCK // 4)) + rows // (Q_BLOCK // 4)
    ik = jnp.arange(2 * Q_BLOCK)[None, :]
    def bias(iq, shift):
        dist = iq[:, None] + shift - ik
        return jnp.where((dist >= 0) & (dist <= Q_BLOCK), 0.0, MASKED).astype(F32)
    return jnp.stack([bias(rows_mod4, 0), bias(rows_mod4, Q_BLOCK), bias(rows, 0), bias(rows, Q_BLOCK)])


def kernel(x, norm_g, w_in, q_norm_g, k_norm_g, dw_kernel, dw_bias, conv_ln_g, conv_ln_b, w_pw,
           att_out_g, conv_out_g, w_out):
    B, S, D = x.shape
    assert D == D_MODEL and S % (16 * 2 * Q_BLOCK) == 0
    depth = norm_g.shape[0]
    cos2, sin2 = _rope_tables(S)
    bias = _mask_bias()
    row = lambda v: v.reshape(1, -1)
    x2 = x.reshape(B * S, D)
    w_in_bf, w_pw_bf, w_out_bf = w_in.astype(BF16), w_pw.astype(BF16), w_out.astype(BF16)
    for layer in range(depth):
        proj2 = _norm_proj(x2, row(norm_g[layer]), w_in_bf, layer)
        proj3 = proj2.reshape(B, S, D_IN)
        att = _attention(proj3, cos2, sin2, row(q_norm_g[layer]), row(k_norm_g[layer]), bias)
        conv_y = _conv_module(proj3, dw_kernel[layer], row(dw_bias[layer]), row(conv_ln_g[layer]),
                              row(conv_ln_b[layer]), w_pw_bf, row(conv_out_g[layer]), layer)
        x2 = _out_proj(att.reshape(B * S, D_ATT), proj2, conv_y.reshape(B * S, D_CONV), x2,
                       row(att_out_g[layer]), w_out_bf, layer)
    return x2.reshape(B, S, D)
```

```python
import functools

import jax
import jax.numpy as jnp
from jax import lax
from jax.experimental import pallas as pl
from jax.experimental.pallas import tpu as pltpu

D_MODEL = 2048
N_HEADS = 16
HEAD_DIM = 128
D_ATT = N_HEADS * HEAD_DIM
D_CONV = D_MODEL
D_MIX = D_ATT + D_CONV
D_IN = 4 * D_ATT + 3 * D_CONV
CONV_WIDTH = 31
DILATIONS = (1, 4, 16)
Q_BLOCK = 128
ROPE_THETA = 10000.0
EPS = 1e-6
MASKED = -1e30

SUBLANES = 8

VMEM_LIMIT_BYTES = 56 * 1024 * 1024

F32 = jnp.float32
BF16 = jnp.bfloat16


def _sigmoid(x):
    return 0.5 * jnp.tanh(0.5 * x) + 0.5


def _silu(x):
    h = 0.5 * x
    return h * jnp.tanh(h) + h


def _norm_proj_kernel(x_ref, g_ref, w_ref, o_ref, h_ref, *, row_chunk):
    tm = x_ref.shape[0]

    @pl.when(pl.program_id(1) == 0)
    def _():
        def body(i, carry):
            rows = pl.ds(pl.multiple_of(i * row_chunk, row_chunk), row_chunk)
            x = x_ref[rows, :]
            y = x * lax.rsqrt(jnp.mean(x * x, axis=-1, keepdims=True) + EPS)
            h_ref[rows, :] = (y * g_ref[...]).astype(BF16)
            return carry
        lax.fori_loop(0, tm // row_chunk, body, 0)

    o_ref[...] = jnp.dot(h_ref[...], w_ref[...], preferred_element_type=F32).astype(o_ref.dtype)


def _norm_proj(x2, g, w_bf, layer, *, tm=1024, tn=1024):
    T = x2.shape[0]
    return pl.pallas_call(
        functools.partial(_norm_proj_kernel, row_chunk=128),
        out_shape=jax.ShapeDtypeStruct((T, D_IN), BF16),
        grid=(T // tm, D_IN // tn),
        in_specs=[
            pl.BlockSpec((tm, D_MODEL), lambda i, j: (i, 0)),
            pl.BlockSpec((1, D_MODEL), lambda i, j: (0, 0)),
            pl.BlockSpec((None, D_MODEL, tn), lambda i, j: (layer, 0, j)),
        ],
        out_specs=pl.BlockSpec((tm, tn), lambda i, j: (i, j)),
        scratch_shapes=[pltpu.VMEM((tm, D_MODEL), BF16)],
        compiler_params=pltpu.CompilerParams(
            dimension_semantics=("arbitrary", "arbitrary"), vmem_limit_bytes=VMEM_LIMIT_BYTES),
        name="norm_proj",
    )(x2, g, w_bf)


def _proj_kernel(h_ref, w_ref, o_ref):
    o_ref[...] = jnp.dot(h_ref[...], w_ref[...], preferred_element_type=F32).astype(o_ref.dtype)


def _proj(h2, w_bf, layer, *, tm=2048, tn=1024):
    T = h2.shape[0]
    return pl.pallas_call(
        _proj_kernel,
        out_shape=jax.ShapeDtypeStruct((T, D_IN), BF16),
        grid=(T // tm, D_IN // tn),
        in_specs=[
            pl.BlockSpec((tm, D_MODEL), lambda i, j: (i, 0)),
            pl.BlockSpec((None, D_MODEL, tn), lambda i, j: (layer, 0, j)),
        ],
        out_specs=pl.BlockSpec((tm, tn), lambda i, j: (i, j)),
        compiler_params=pltpu.CompilerParams(
            dimension_semantics=("arbitrary", "arbitrary"), vmem_limit_bytes=VMEM_LIMIT_BYTES),
        name="proj",
    )(h2, w_bf)


NAT, MOD4, MOD16 = 0, 1, 2


def _attn_kernel(q_ref, k_ref, v_ref, cos_ref, sin_ref, gq_ref, gk_ref, bias_ref, o_ref,
                 qs, ks, vs, t0, t1, *stats, seq, group):
    obuf, lbuf, mbuf = stats[0:3], stats[3:6], stats[6:9]
    chunk = 256
    quarter = seq // 4
    sixteenth = seq // 16

    @pl.when((pl.program_id(0) == 0) & (pl.program_id(1) == 0))
    def _():
        for layout in range(3):
            vs[layout, :, HEAD_DIM:2 * HEAD_DIM] = jnp.ones((seq, HEAD_DIM), BF16)

    def prepare(src_ref, g_ref, dst, first_layout, normalize, scale):
        if normalize:
            gain = g_ref[...] * scale
            gain_rot = pltpu.roll(gain, HEAD_DIM // 2, 1)

        def body(i, carry):
            rows = pl.ds(pl.multiple_of(i * chunk, chunk), chunk)
            raw = src_ref[rows, :]
            y = raw.astype(F32)
            if normalize:
                rot = pltpu.roll(raw, HEAD_DIM // 2, 1).astype(F32)
                inv = lax.rsqrt(jnp.mean(y * y, axis=-1, keepdims=True) + EPS)
                y = (y * gain * cos_ref[rows, :] + rot * gain_rot * sin_ref[rows, :]) * inv
            t0[rows, :] = y
            if first_layout == NAT:
                dst[NAT, rows, 0:HEAD_DIM] = y.astype(BF16)
            return carry
        lax.fori_loop(0, seq // chunk, body, 0, unroll=4)

        for r4 in range(4):
            def body4(i, carry, r4=r4):
                y = t0[pl.ds(r4 + i * (4 * chunk), chunk, stride=4), :]
                rows = pl.ds(pl.multiple_of(r4 * quarter + i * chunk, chunk), chunk)
                t1[rows, :] = y
                dst[MOD4 - first_layout, rows, 0:HEAD_DIM] = y.astype(BF16)
                return carry
            lax.fori_loop(0, quarter // chunk, body4, 0)

        for rp in range(4):
            for r4 in range(4):
                y = t1[pl.ds(r4 * quarter + rp, sixteenth, stride=4), :]
                dst[MOD16 - first_layout, pl.ds((4 * rp + r4) * sixteenth, sixteenth), 0:HEAD_DIM] = y.astype(BF16)

    prepare(q_ref, gq_ref, qs, MOD4, True, HEAD_DIM ** -0.5)
    prepare(k_ref, gk_ref, ks, NAT, True, 1.0)
    prepare(v_ref, None, vs, NAT, False, 1.0)

    n_blocks = seq // Q_BLOCK
    piece = Q_BLOCK // 4
    for br, d in enumerate(DILATIONS):
        nb = n_blocks // d

        def one_block(c, br=br, d=d, nb=nb):
            n = lax.rem(c, nb)
            first = n == 0
            q_row = pl.multiple_of(c * Q_BLOCK, Q_BLOCK)
            k_row = pl.multiple_of(jnp.where(first, c * Q_BLOCK, (c - 1) * Q_BLOCK), Q_BLOCK)
            if d == 1:
                runs = [pl.ds(pl.multiple_of(r4 * quarter + c * piece, piece), piece) for r4 in range(4)]
                qb = jnp.concatenate([qs[0, run, :] for run in runs], axis=0)
            else:
                qb = qs[br - 1, pl.ds(q_row, Q_BLOCK), :]
            kb = ks[br, pl.ds(k_row, 2 * Q_BLOCK), :]
            vb = vs[br, pl.ds(k_row, 2 * Q_BLOCK), :]
            s = lax.dot_general(qb, kb, (((1,), (1,)), ((), ())), preferred_element_type=F32)
            s = s + bias_ref[jnp.where(first, 0, 1) + (0 if d == 1 else 2)]
            m = jnp.max(s, axis=-1, keepdims=True)
            p = jnp.exp(s - m).astype(BF16)
            oe = jnp.dot(p, vb, preferred_element_type=F32)
            mb = jnp.broadcast_to(m, (Q_BLOCK, HEAD_DIM))
            ob, lb = oe[:, 0:HEAD_DIM], oe[:, HEAD_DIM:2 * HEAD_DIM]
            if d == 1:
                for r4, run in enumerate(runs):
                    sub = slice(r4 * piece, (r4 + 1) * piece)
                    obuf[br][run, :] = ob[sub, :]
                    lbuf[br][run, :] = lb[sub, :]
                    mbuf[br][run, :] = mb[sub, :]
            else:
                if d == 4:
                    rows = pl.ds(q_row, Q_BLOCK)
                else:
                    r = c // nb
                    rows = pl.ds((r % 4) * quarter + r // 4 + n * (4 * Q_BLOCK), Q_BLOCK, stride=4)
                obuf[br][rows, :] = ob
                lbuf[br][rows, :] = lb
                mbuf[br][rows, :] = mb

        def block_group(it, carry, one_block=one_block):
            for g in range(group):
                one_block(it * group + g)
            return carry
        lax.fori_loop(0, n_blocks // group, block_group, 0)

    for r4 in range(4):
        def finish(i, carry, r4=r4):
            rows = pl.ds(pl.multiple_of(r4 * quarter + i * chunk, chunk), chunk)
            ms = [mbuf[br][rows, :] for br in range(3)]
            m_all = jnp.maximum(jnp.maximum(ms[0], ms[1]), ms[2])
            num = den = None
            for br in range(3):
                w = jnp.exp(ms[br] - m_all)
                t_num = w * obuf[br][rows, :]
                t_den = w * lbuf[br][rows, :]
                num = t_num if num is None else num + t_num
                den = t_den if den is None else den + t_den
            o_ref[pl.ds(r4 + i * (4 * chunk), chunk, stride=4), :] = (num / den).astype(o_ref.dtype)
            return carry
        lax.fori_loop(0, quarter // chunk, finish, 0)


def _attention(proj3, cos2, sin2, gq, gk, bias, *, group=32):
    B, S, _ = proj3.shape
    head_spec = lambda off: pl.BlockSpec((None, S, HEAD_DIM), lambda b, h: (b, 0, off + h))
    const = lambda shape: pl.BlockSpec(shape, lambda b, h: (0,) * len(shape), pipeline_mode=pl.Buffered(1))
    return pl.pallas_call(
        functools.partial(_attn_kernel, seq=S, group=group),
        out_shape=jax.ShapeDtypeStruct((B, S, D_ATT), F32),
        grid=(B, N_HEADS),
        in_specs=[
            head_spec(0), head_spec(N_HEADS), head_spec(2 * N_HEADS),
            const((S, HEAD_DIM)), const((S, HEAD_DIM)),
            const((1, HEAD_DIM)), const((1, HEAD_DIM)),
            const((4, Q_BLOCK, 2 * Q_BLOCK)),
        ],
        out_specs=pl.BlockSpec((None, S, HEAD_DIM), lambda b, h: (b, 0, h)),
        scratch_shapes=[
            pltpu.VMEM((2, S, HEAD_DIM), BF16), pltpu.VMEM((3, S, HEAD_DIM), BF16),
            pltpu.VMEM((3, S, 2 * HEAD_DIM), BF16),
            pltpu.VMEM((S, HEAD_DIM), F32), pltpu.VMEM((S, HEAD_DIM), F32),
        ] + [pltpu.VMEM((S, HEAD_DIM), F32)] * 9,
        compiler_params=pltpu.CompilerParams(
            dimension_semantics=("arbitrary", "arbitrary"), vmem_limit_bytes=VMEM_LIMIT_BYTES),
        name="dilated_attention",
    )(proj3, proj3, proj3, cos2, sin2, gq, gk, bias)


CONV_HALO = 32


def _conv_kernel(a_ref, b_ref, gc_ref, dw_ref, dwb_ref, lng_ref, lnb_ref, wpw_ref, og_ref, o_ref,
                 u_ext, y_s, z_s, *, row_chunk, conv_rows):
    tm = a_ref.shape[0]

    def row_loop(body, n_rows):
        def step(i, carry):
            body(pl.ds(pl.multiple_of(i * n_rows, n_rows), n_rows),
                 pl.ds(pl.multiple_of(i * n_rows + CONV_HALO, SUBLANES), n_rows))
            return carry
        lax.fori_loop(0, tm // n_rows, step, 0)

    @pl.when(pl.program_id(1) == 0)
    def _():
        u_ext[0:CONV_HALO, :] = jnp.zeros((CONV_HALO, D_CONV), F32)

    def glu(rows, ext_rows):
        u_ext[ext_rows, :] = a_ref[rows, :].astype(F32) * _sigmoid(b_ref[rows, :].astype(F32))
    row_loop(glu, 2 * row_chunk)

    lead = CONV_HALO - (CONV_WIDTH - 1)

    def lanes_body(cc, carry):
        lanes = pl.ds(pl.multiple_of(cc * 128, 128), 128)
        for rc in range(tm // conv_rows):
            r0 = rc * conv_rows
            acc = jnp.broadcast_to(dwb_ref[:, lanes], (conv_rows, 128))
            for shift in range(SUBLANES):
                n_rows = conv_rows + (SUBLANES if shift else 0)
                part = None
                for off in range(shift, lead + CONV_WIDTH, SUBLANES):
                    if off < lead:
                        continue
                    start = r0 + off - shift
                    term = dw_ref[off - lead:off - lead + 1, lanes] * u_ext[start:start + n_rows, lanes]
                    part = term if part is None else part + term
                acc = acc + part[shift:shift + conv_rows, :]
            y_s[r0:r0 + conv_rows, lanes] = acc
        return carry
    lax.fori_loop(0, D_CONV // 128, lanes_body, 0)

    u_ext[0:CONV_HALO, :] = u_ext[tm:tm + CONV_HALO, :]

    def layer_norm_swish(rows, ext_rows):
        y = y_s[rows, :]
        yc = y - jnp.mean(y, axis=-1, keepdims=True)
        yn = yc * lax.rsqrt(jnp.mean(yc * yc, axis=-1, keepdims=True) + EPS)
        z_s[rows, :] = _silu(yn * lng_ref[...] + lnb_ref[...]).astype(BF16)
    row_loop(layer_norm_swish, 2 * row_chunk)

    y_s[...] = jnp.dot(z_s[...], wpw_ref[...], preferred_element_type=F32)

    def norm_gate(rows, ext_rows):
        conv = y_s[rows, :]
        cn = conv * lax.rsqrt(jnp.mean(conv * conv, axis=-1, keepdims=True) + EPS) * og_ref[...]
        o_ref[rows, :] = (cn * _silu(gc_ref[rows, :].astype(F32))).astype(o_ref.dtype)
    row_loop(norm_gate, row_chunk)


def _conv_module(proj3, dw, dwb, lng, lnb, wpw_bf, og, layer, *, tm=512):
    B, S, _ = proj3.shape
    cb = D_CONV // D_CONV
    col = lambda blk: pl.BlockSpec((None, tm, D_CONV), lambda b, i: (b, i, blk))
    const = lambda shape: pl.BlockSpec(shape, lambda b, i: (0,) * len(shape), pipeline_mode=pl.Buffered(1))
    base = 4 * D_ATT // D_CONV
    return pl.pallas_call(
        functools.partial(_conv_kernel, row_chunk=64, conv_rows=128),
        out_shape=jax.ShapeDtypeStruct((B, S, D_CONV), BF16),
        grid=(B, S // tm),
        in_specs=[
            col(base), col(base + cb), col(base + 2 * cb),
            const((CONV_WIDTH, D_CONV)), const((1, D_CONV)), const((1, D_CONV)), const((1, D_CONV)),
            pl.BlockSpec((None, D_CONV, D_CONV), lambda b, i: (layer, 0, 0), pipeline_mode=pl.Buffered(1)),
            const((1, D_CONV)),
        ],
        out_specs=pl.BlockSpec((None, tm, D_CONV), lambda b, i: (b, i, 0)),
        scratch_shapes=[pltpu.VMEM((tm + CONV_HALO, D_CONV), F32), pltpu.VMEM((tm, D_CONV), F32),
                        pltpu.VMEM((tm, D_CONV), BF16)],
        compiler_params=pltpu.CompilerParams(
            dimension_semantics=("arbitrary", "arbitrary"), vmem_limit_bytes=VMEM_LIMIT_BYTES),
        name="conformer_conv",
    )(proj3, proj3, proj3, dw, dwb, lng, lnb, wpw_bf, og)


def _out_kernel(att_ref, ga_ref, cy_ref, x_ref, g_ref, w_ref, *rest, row_chunk, n_tiles, emit_h):
    if emit_h:
        gn_ref, o_ref, h_ref, lhs_a, lhs_b = rest
    else:
        o_ref, lhs_a, lhs_b = rest
    s = pl.program_id(0)
    tm = att_ref.shape[0]

    def gate_rows(dst, rows):
        a = att_ref[rows, :]
        an = a * lax.rsqrt(jnp.mean(a * a, axis=-1, keepdims=True) + EPS) * g_ref[...]
        dst[rows, 0:D_ATT] = (an * _silu(ga_ref[rows, :].astype(F32))).astype(BF16)
        dst[rows, D_ATT:D_MIX] = cy_ref[rows, :]

    def project(src):
        o_ref[...] = x_ref[...] + jnp.dot(src[...], w_ref[...], preferred_element_type=F32)
        if emit_h:
            for i in range(tm // row_chunk):
                rows = pl.ds(i * row_chunk, row_chunk)
                xn = o_ref[rows, :]
                hn = xn * lax.rsqrt(jnp.mean(xn * xn, axis=-1, keepdims=True) + EPS)
                h_ref[rows, :] = (hn * gn_ref[...]).astype(BF16)

    @pl.when(s == 0)
    def _():
        def body(i, carry):
            gate_rows(lhs_a, pl.ds(pl.multiple_of(i * row_chunk, row_chunk), row_chunk))
            return carry
        lax.fori_loop(0, tm // row_chunk, body, 0)

    def steady(src, dst):
        project(src)
        for i in range(tm // row_chunk):
            gate_rows(dst, pl.ds(i * row_chunk, row_chunk))

    @pl.when((s > 0) & (s < n_tiles) & (lax.rem(s, 2) == 1))
    def _():
        steady(lhs_a, lhs_b)

    @pl.when((s > 0) & (s < n_tiles) & (lax.rem(s, 2) == 0))
    def _():
        steady(lhs_b, lhs_a)

    @pl.when(s == n_tiles)
    def _():
        project(lhs_a if n_tiles % 2 == 1 else lhs_b)


def _out_proj(att2, proj2, convy2, x2, g, w_bf, layer, g_next=None, *, tm=256):
    T = x2.shape[0]
    emit_h = g_next is not None
    n_tiles = T // tm
    gate_blk = 3 * D_ATT // D_ATT
    ahead = lambda blk: (lambda s: (jnp.minimum(s, n_tiles - 1), blk))
    behind = lambda s: (jnp.maximum(s - 1, 0), 0)
    return pl.pallas_call(
        functools.partial(_out_kernel, row_chunk=16, n_tiles=n_tiles, emit_h=emit_h),
        out_shape=[jax.ShapeDtypeStruct((T, D_MODEL), F32)] + [jax.ShapeDtypeStruct((T, D_MODEL), BF16)] * emit_h,
        grid=(n_tiles + 1,),
        in_specs=[
            pl.BlockSpec((tm, D_ATT), ahead(0)),
            pl.BlockSpec((tm, D_ATT), ahead(gate_blk)),
            pl.BlockSpec((tm, D_CONV), ahead(0)),
            pl.BlockSpec((tm, D_MODEL), behind),
            pl.BlockSpec((1, D_ATT), lambda s: (0, 0), pipeline_mode=pl.Buffered(1)),
            pl.BlockSpec((None, D_MIX, D_MODEL), lambda s: (layer, 0, 0), pipeline_mode=pl.Buffered(1)),
        ] + [pl.BlockSpec((1, D_MODEL), lambda s: (0, 0), pipeline_mode=pl.Buffered(1))] * emit_h,
        out_specs=[pl.BlockSpec((tm, D_MODEL), behind)] * (1 + emit_h),
        scratch_shapes=[pltpu.VMEM((tm, D_MIX), BF16), pltpu.VMEM((tm, D_MIX), BF16)],
        compiler_params=pltpu.CompilerParams(
            dimension_semantics=("arbitrary",), vmem_limit_bytes=VMEM_LIMIT_BYTES),
        name="out_proj",
    )(att2, proj2, convy2, x2, g, w_bf, *([g_next] if emit_h else []))


def _rope_tables(seq):
    inv_freq = 1.0 / (ROPE_THETA ** (jnp.arange(0, HEAD_DIM, 2, dtype=F32) / HEAD_DIM))
    ang = jnp.arange(seq, dtype=F32)[:, None] * inv_freq[None, :]
    cos, sin = jnp.cos(ang), jnp.sin(ang)
    return jnp.concatenate([cos, cos], axis=-1), jnp.concatenate([-sin, sin], axis=-1)


def _mask_bias():
    rows = jnp.arange(Q_BLOCK)
    rows_mod4 = 4 * (rows % (Q_BLOCK // 4)) + rows // (Q_BLOCK // 4)
    ik = jnp.arange(2 * Q_BLOCK)[None, :]
    def bias(iq, shift):
        dist = iq[:, None] + shift - ik
        return jnp.where((dist >= 0) & (dist <= Q_BLOCK), 0.0, MASKED).astype(F32)
    return jnp.stack([bias(rows_mod4, 0), bias(rows_mod4, Q_BLOCK), bias(rows, 0), bias(rows, Q_BLOCK)])


def kernel(x, norm_g, w_in, q_norm_g, k_norm_g, dw_kernel, dw_bias, conv_ln_g, conv_ln_b, w_pw,
           att_out_g, conv_out_g, w_out):
    B, S, D = x.shape
    assert D == D_MODEL and S % (16 * 2 * Q_BLOCK) == 0
    depth = norm_g.shape[0]
    cos2, sin2 = _rope_tables(S)
    bias = _mask_bias()
    row = lambda v: v.reshape(1, -1)
    x2 = x.reshape(B * S, D)
    w_in_bf, w_pw_bf, w_out_bf = w_in.astype(BF16), w_pw.astype(BF16), w_out.astype(BF16)
    h2 = None
    for layer in range(depth):
        if h2 is None:
            proj2 = _norm_proj(x2, row(norm_g[layer]), w_in_bf, layer)
        else:
            proj2 = _proj(h2, w_in_bf, layer)
        proj3 = proj2.reshape(B, S, D_IN)
        att = _attention(proj3, cos2, sin2, row(q_norm_g[layer]), row(k_norm_g[layer]), bias)
        conv_y = _conv_module(proj3, dw_kernel[layer], row(dw_bias[layer]), row(conv_ln_g[layer]),
                              row(conv_ln_b[layer]), w_pw_bf, row(conv_out_g[layer]), layer)
        g_next = row(norm_g[layer + 1]) if layer + 1 < depth else None
        outs = _out_proj(att.reshape(B * S, D_ATT), proj2, conv_y.reshape(B * S, D_CONV), x2,
                         row(att_out_g[layer]), w_out_bf, layer, g_next)
        x2, h2 = outs if g_next is not None else (outs[0], None)
    return x2.reshape(B, S, D)
```

```python
import functools

import jax
import jax.numpy as jnp
from jax import lax
from jax.experimental import pallas as pl
from jax.experimental.pallas import tpu as pltpu

D_MODEL = 2048
N_HEADS = 16
HEAD_DIM = 128
D_ATT = N_HEADS * HEAD_DIM
D_CONV = D_MODEL
D_MIX = D_ATT + D_CONV
D_IN = 4 * D_ATT + 3 * D_CONV
CONV_WIDTH = 31
DILATIONS = (1, 4, 16)
Q_BLOCK = 128
ROPE_THETA = 10000.0
EPS = 1e-6
MASKED = -1e30

SUBLANES = 8

VMEM_LIMIT_BYTES = 56 * 1024 * 1024

F32 = jnp.float32
BF16 = jnp.bfloat16


def _sigmoid(x):
    return 0.5 * jnp.tanh(0.5 * x) + 0.5


def _silu(x):
    h = 0.5 * x
    return h * jnp.tanh(h) + h


def _norm_proj_kernel(x_ref, g_ref, w_ref, o_ref, h_ref, *, row_chunk):
    tm = x_ref.shape[0]

    @pl.when(pl.program_id(1) == 0)
    def _():
        def body(i, carry):
            rows = pl.ds(pl.multiple_of(i * row_chunk, row_chunk), row_chunk)
            x = x_ref[rows, :]
            y = x * lax.rsqrt(jnp.mean(x * x, axis=-1, keepdims=True) + EPS)
            h_ref[rows, :] = (y * g_ref[...]).astype(BF16)
            return carry
        lax.fori_loop(0, tm // row_chunk, body, 0)

    o_ref[...] = jnp.dot(h_ref[...], w_ref[...], preferred_element_type=F32).astype(o_ref.dtype)


def _norm_proj(x2, g, w_bf, layer, *, tm=1024, tn=1024):
    T = x2.shape[0]
    return pl.pallas_call(
        functools.partial(_norm_proj_kernel, row_chunk=128),
        out_shape=jax.ShapeDtypeStruct((T, D_IN), BF16),
        grid=(T // tm, D_IN // tn),
        in_specs=[
            pl.BlockSpec((tm, D_MODEL), lambda i, j: (i, 0)),
            pl.BlockSpec((1, D_MODEL), lambda i, j: (0, 0)),
            pl.BlockSpec((None, D_MODEL, tn), lambda i, j: (layer, 0, j)),
        ],
        out_specs=pl.BlockSpec((tm, tn), lambda i, j: (i, j)),
        scratch_shapes=[pltpu.VMEM((tm, D_MODEL), BF16)],
        compiler_params=pltpu.CompilerParams(
            dimension_semantics=("arbitrary", "arbitrary"), vmem_limit_bytes=VMEM_LIMIT_BYTES),
        name="norm_proj",
    )(x2, g, w_bf)


def _proj_kernel(h_ref, w_ref, o_ref):
    o_ref[...] = jnp.dot(h_ref[...], w_ref[...], preferred_element_type=F32).astype(o_ref.dtype)


def _proj(h2, w_bf, layer, *, tm=2048, tn=1024):
    T = h2.shape[0]
    return pl.pallas_call(
        _proj_kernel,
        out_shape=jax.ShapeDtypeStruct((T, D_IN), BF16),
        grid=(T // tm, D_IN // tn),
        in_specs=[
            pl.BlockSpec((tm, D_MODEL), lambda i, j: (i, 0)),
            pl.BlockSpec((None, D_MODEL, tn), lambda i, j: (layer, 0, j)),
        ],
        out_specs=pl.BlockSpec((tm, tn), lambda i, j: (i, j)),
        compiler_params=pltpu.CompilerParams(
            dimension_semantics=("arbitrary", "arbitrary"), vmem_limit_bytes=VMEM_LIMIT_BYTES),
        name="proj",
    )(h2, w_bf)


NAT, MOD4, MOD16 = 0, 1, 2


def _attn_kernel(q_ref, k_ref, v_ref, cos_ref, sin_ref, gq_ref, gk_ref, bias_ref, o_ref,
                 qs, ks, vs, t0, t1, *stats, seq, group):
    obuf, lbuf, mbuf = stats[0:3], stats[3:6], stats[6:9]
    chunk = 256
    quarter = seq // 4
    sixteenth = seq // 16

    @pl.when((pl.program_id(0) == 0) & (pl.program_id(1) == 0))
    def _():
        for layout in range(3):
            vs[layout, :, HEAD_DIM:2 * HEAD_DIM] = jnp.ones((seq, HEAD_DIM), BF16)

    def prepare(src_ref, g_ref, dst, first_layout, normalize, scale):
        if normalize:
            gain = g_ref[...] * scale
            gain_rot = pltpu.roll(gain, HEAD_DIM // 2, 1)

        def body(i, carry):
            rows = pl.ds(pl.multiple_of(i * chunk, chunk), chunk)
            raw = src_ref[rows, :]
            y = raw.astype(F32)
            if normalize:
                rot = pltpu.roll(raw, HEAD_DIM // 2, 1).astype(F32)
                inv = lax.rsqrt(jnp.mean(y * y, axis=-1, keepdims=True) + EPS)
                y = (y * gain * cos_ref[rows, :] + rot * gain_rot * sin_ref[rows, :]) * inv
            t0[rows, :] = y
            if first_layout == NAT:
                dst[NAT, rows, 0:HEAD_DIM] = y.astype(BF16)
            return carry
        lax.fori_loop(0, seq // chunk, body, 0, unroll=8)

        for r4 in range(4):
            def body4(i, carry, r4=r4):
                y = t0[pl.ds(r4 + i * (4 * chunk), chunk, stride=4), :]
                rows = pl.ds(pl.multiple_of(r4 * quarter + i * chunk, chunk), chunk)
                t1[rows, :] = y
                dst[MOD4 - first_layout, rows, 0:HEAD_DIM] = y.astype(BF16)
                return carry
            lax.fori_loop(0, quarter // chunk, body4, 0)

        for rp in range(4):
            for r4 in range(4):
                y = t1[pl.ds(r4 * quarter + rp, sixteenth, stride=4), :]
                dst[MOD16 - first_layout, pl.ds((4 * rp + r4) * sixteenth, sixteenth), 0:HEAD_DIM] = y.astype(BF16)

    prepare(q_ref, gq_ref, qs, MOD4, True, HEAD_DIM ** -0.5)
    prepare(k_ref, gk_ref, ks, NAT, True, 1.0)
    prepare(v_ref, None, vs, NAT, False, 1.0)

    n_blocks = seq // Q_BLOCK
    piece = Q_BLOCK // 4
    for br, d in enumerate(DILATIONS):
        nb = n_blocks // d

        def one_block(c, br=br, d=d, nb=nb):
            n = lax.rem(c, nb)
            first = n == 0
            q_row = pl.multiple_of(c * Q_BLOCK, Q_BLOCK)
            k_row = pl.multiple_of(jnp.where(first, c * Q_BLOCK, (c - 1) * Q_BLOCK), Q_BLOCK)
            if d == 1:
                runs = [pl.ds(pl.multiple_of(r4 * quarter + c * piece, piece), piece) for r4 in range(4)]
                qb = jnp.concatenate([qs[0, run, :] for run in runs], axis=0)
            else:
                qb = qs[br - 1, pl.ds(q_row, Q_BLOCK), :]
            kb = ks[br, pl.ds(k_row, 2 * Q_BLOCK), :]
            vb = vs[br, pl.ds(k_row, 2 * Q_BLOCK), :]
            s = lax.dot_general(qb, kb, (((1,), (1,)), ((), ())), preferred_element_type=F32)
            s = s + bias_ref[jnp.where(first, 0, 1) + (0 if d == 1 else 2)]
            m = jnp.max(s, axis=-1, keepdims=True)
            p = jnp.exp(s - m).astype(BF16)
            oe = jnp.dot(p, vb, preferred_element_type=F32)
            mb = jnp.broadcast_to(m, (Q_BLOCK, HEAD_DIM))
            ob, lb = oe[:, 0:HEAD_DIM], oe[:, HEAD_DIM:2 * HEAD_DIM]
            if d == 1:
                for r4, run in enumerate(runs):
                    sub = slice(r4 * piece, (r4 + 1) * piece)
                    obuf[br][run, :] = ob[sub, :]
                    lbuf[br][run, :] = lb[sub, :]
                    mbuf[br][run, :] = mb[sub, :]
            else:
                if d == 4:
                    rows = pl.ds(q_row, Q_BLOCK)
                else:
                    r = c // nb
                    rows = pl.ds((r % 4) * quarter + r // 4 + n * (4 * Q_BLOCK), Q_BLOCK, stride=4)
                obuf[br][rows, :] = ob
                lbuf[br][rows, :] = lb
                mbuf[br][rows, :] = mb

        def block_group(it, carry, one_block=one_block):
            for g in range(group):
                one_block(it * group + g)
            return carry
        lax.fori_loop(0, n_blocks // group, block_group, 0)

    for r4 in range(4):
        def finish(i, carry, r4=r4):
            rows = pl.ds(pl.multiple_of(r4 * quarter + i * chunk, chunk), chunk)
            ms = [mbuf[br][rows, :] for br in range(3)]
            m_all = jnp.maximum(jnp.maximum(ms[0], ms[1]), ms[2])
            num = den = None
            for br in range(3):
                w = jnp.exp(ms[br] - m_all)
                t_num = w * obuf[br][rows, :]
                t_den = w * lbuf[br][rows, :]
                num = t_num if num is None else num + t_num
                den = t_den if den is None else den + t_den
            o_ref[pl.ds(r4 + i * (4 * chunk), chunk, stride=4), :] = (num / den).astype(o_ref.dtype)
            return carry
        lax.fori_loop(0, quarter // chunk, finish, 0, unroll=2)


def _attention(proj3, cos2, sin2, gq, gk, bias, *, group=32):
    B, S, _ = proj3.shape
    head_spec = lambda off: pl.BlockSpec((None, S, HEAD_DIM), lambda b, h: (b, 0, off + h))
    const = lambda shape: pl.BlockSpec(shape, lambda b, h: (0,) * len(shape), pipeline_mode=pl.Buffered(1))
    return pl.pallas_call(
        functools.partial(_attn_kernel, seq=S, group=group),
        out_shape=jax.ShapeDtypeStruct((B, S, D_ATT), F32),
        grid=(B, N_HEADS),
        in_specs=[
            head_spec(0), head_spec(N_HEADS), head_spec(2 * N_HEADS),
            const((S, HEAD_DIM)), const((S, HEAD_DIM)),
            const((1, HEAD_DIM)), const((1, HEAD_DIM)),
            const((4, Q_BLOCK, 2 * Q_BLOCK)),
        ],
        out_specs=pl.BlockSpec((None, S, HEAD_DIM), lambda b, h: (b, 0, h)),
        scratch_shapes=[
            pltpu.VMEM((2, S, HEAD_DIM), BF16), pltpu.VMEM((3, S, HEAD_DIM), BF16),
            pltpu.VMEM((3, S, 2 * HEAD_DIM), BF16),
            pltpu.VMEM((S, HEAD_DIM), F32), pltpu.VMEM((S, HEAD_DIM), F32),
        ] + [pltpu.VMEM((S, HEAD_DIM), F32)] * 9,
        compiler_params=pltpu.CompilerParams(
            dimension_semantics=("arbitrary", "arbitrary"), vmem_limit_bytes=VMEM_LIMIT_BYTES),
        name="dilated_attention",
    )(proj3, proj3, proj3, cos2, sin2, gq, gk, bias)


CONV_HALO = 32


def _conv_kernel(a_ref, b_ref, dw_ref, dwb_ref, lng_ref, lnb_ref, o_ref, u_ext, y_s, *, row_chunk, conv_rows):
    tm = a_ref.shape[0]

    def row_loop(body, n_rows):
        def step(i, carry):
            body(pl.ds(pl.multiple_of(i * n_rows, n_rows), n_rows),
                 pl.ds(pl.multiple_of(i * n_rows + CONV_HALO, SUBLANES), n_rows))
            return carry
        lax.fori_loop(0, tm // n_rows, step, 0)

    @pl.when(pl.program_id(1) == 0)
    def _():
        u_ext[0:CONV_HALO, :] = jnp.zeros((CONV_HALO, D_CONV), F32)

    def glu(rows, ext_rows):
        u_ext[ext_rows, :] = a_ref[rows, :].astype(F32) * _sigmoid(b_ref[rows, :].astype(F32))
    row_loop(glu, 2 * row_chunk)

    lead = CONV_HALO - (CONV_WIDTH - 1)

    def lanes_body(cc, carry):
        lanes = pl.ds(pl.multiple_of(cc * 128, 128), 128)
        for rc in range(tm // conv_rows):
            r0 = rc * conv_rows
            acc = jnp.broadcast_to(dwb_ref[:, lanes], (conv_rows, 128))
            for shift in range(SUBLANES):
                n_rows = conv_rows + (SUBLANES if shift else 0)
                part = None
                for off in range(shift, lead + CONV_WIDTH, SUBLANES):
                    if off < lead:
                        continue
                    start = r0 + off - shift
                    term = dw_ref[off - lead:off - lead + 1, lanes] * u_ext[start:start + n_rows, lanes]
                    part = term if part is None else part + term
                acc = acc + part[shift:shift + conv_rows, :]
            y_s[r0:r0 + conv_rows, lanes] = acc
        return carry
    lax.fori_loop(0, D_CONV // 128, lanes_body, 0)

    u_ext[0:CONV_HALO, :] = u_ext[tm:tm + CONV_HALO, :]

    def layer_norm_swish(rows, ext_rows):
        y = y_s[rows, :]
        yc = y - jnp.mean(y, axis=-1, keepdims=True)
        yn = yc * lax.rsqrt(jnp.mean(yc * yc, axis=-1, keepdims=True) + EPS)
        o_ref[rows, :] = _silu(yn * lng_ref[...] + lnb_ref[...]).astype(o_ref.dtype)
    row_loop(layer_norm_swish, 2 * row_chunk)


def _conv_module(proj3, dw, dwb, lng, lnb, *, tm=512):
    B, S, _ = proj3.shape
    col = lambda blk: pl.BlockSpec((None, tm, D_CONV), lambda b, i: (b, i, blk))
    const = lambda shape: pl.BlockSpec(shape, lambda b, i: (0,) * len(shape), pipeline_mode=pl.Buffered(1))
    base = 4 * D_ATT // D_CONV
    return pl.pallas_call(
        functools.partial(_conv_kernel, row_chunk=64, conv_rows=128),
        out_shape=jax.ShapeDtypeStruct((B, S, D_CONV), BF16),
        grid=(B, S // tm),
        in_specs=[
            col(base), col(base + 1),
            const((CONV_WIDTH, D_CONV)), const((1, D_CONV)), const((1, D_CONV)), const((1, D_CONV)),
        ],
        out_specs=pl.BlockSpec((None, tm, D_CONV), lambda b, i: (b, i, 0)),
        scratch_shapes=[pltpu.VMEM((tm + CONV_HALO, D_CONV), F32), pltpu.VMEM((tm, D_CONV), F32)],
        compiler_params=pltpu.CompilerParams(
            dimension_semantics=("arbitrary", "arbitrary"), vmem_limit_bytes=VMEM_LIMIT_BYTES),
        name="conformer_conv",
    )(proj3, proj3, dw, dwb, lng, lnb)


def _out_kernel(att_ref, ga_ref, z_ref, gc_ref, x_ref, g_ref, og_ref, wpw_ref, w_ref, *rest,
                row_chunk, n_tiles, emit_h):
    if emit_h:
        gn_ref, o_ref, h_ref, lhs_a, lhs_b, y_pw = rest
    else:
        o_ref, lhs_a, lhs_b, y_pw = rest
    s = pl.program_id(0)
    tm = att_ref.shape[0]

    def gate_att_rows(dst, rows):
        a = att_ref[rows, :]
        an = a * lax.rsqrt(jnp.mean(a * a, axis=-1, keepdims=True) + EPS) * g_ref[...]
        dst[rows, 0:D_ATT] = (an * _silu(ga_ref[rows, :].astype(F32))).astype(BF16)

    def pointwise():
        y_pw[...] = jnp.dot(z_ref[...], wpw_ref[...], preferred_element_type=F32)

    def gate_conv_rows(dst, rows):
        c = y_pw[rows, :]
        cn = c * lax.rsqrt(jnp.mean(c * c, axis=-1, keepdims=True) + EPS) * og_ref[...]
        dst[rows, D_ATT:D_MIX] = (cn * _silu(gc_ref[rows, :].astype(F32))).astype(BF16)

    def project(src):
        o_ref[...] = x_ref[...] + jnp.dot(src[...], w_ref[...], preferred_element_type=F32)
        if emit_h:
            for i in range(tm // row_chunk):
                rows = pl.ds(i * row_chunk, row_chunk)
                xn = o_ref[rows, :]
                hn = xn * lax.rsqrt(jnp.mean(xn * xn, axis=-1, keepdims=True) + EPS)
                h_ref[rows, :] = (hn * gn_ref[...]).astype(BF16)

    @pl.when(s == 0)
    def _():
        pointwise()

        def body(i, carry):
            rows = pl.ds(pl.multiple_of(i * row_chunk, row_chunk), row_chunk)
            gate_att_rows(lhs_a, rows)
            gate_conv_rows(lhs_a, rows)
            return carry
        lax.fori_loop(0, tm // row_chunk, body, 0)

    def steady(src, dst):
        pointwise()
        project(src)
        for i in range(tm // row_chunk):
            gate_att_rows(dst, pl.ds(i * row_chunk, row_chunk))
        for i in range(tm // row_chunk):
            gate_conv_rows(dst, pl.ds(i * row_chunk, row_chunk))

    @pl.when((s > 0) & (s < n_tiles) & (lax.rem(s, 2) == 1))
    def _():
        steady(lhs_a, lhs_b)

    @pl.when((s > 0) & (s < n_tiles) & (lax.rem(s, 2) == 0))
    def _():
        steady(lhs_b, lhs_a)

    @pl.when(s == n_tiles)
    def _():
        project(lhs_a if n_tiles % 2 == 1 else lhs_b)


def _out_proj(att2, proj2, z2, x2, g, og, wpw_bf, w_bf, layer, g_next=None, *, tm=256):
    T = x2.shape[0]
    emit_h = g_next is not None
    n_tiles = T // tm
    gate_blk = 3 * D_ATT // D_ATT
    conv_gate_blk = (4 * D_ATT + 2 * D_CONV) // D_CONV
    const = lambda shape: pl.BlockSpec(shape, lambda s: (0,) * len(shape), pipeline_mode=pl.Buffered(1))
    ahead = lambda blk: (lambda s: (jnp.minimum(s, n_tiles - 1), blk))
    behind = lambda s: (jnp.maximum(s - 1, 0), 0)
    return pl.pallas_call(
        functools.partial(_out_kernel, row_chunk=16, n_tiles=n_tiles, emit_h=emit_h),
        out_shape=[jax.ShapeDtypeStruct((T, D_MODEL), F32)] + [jax.ShapeDtypeStruct((T, D_MODEL), BF16)] * emit_h,
        grid=(n_tiles + 1,),
        in_specs=[
            pl.BlockSpec((tm, D_ATT), ahead(0)),
            pl.BlockSpec((tm, D_ATT), ahead(gate_blk)),
            pl.BlockSpec((tm, D_CONV), ahead(0)),
            pl.BlockSpec((tm, D_CONV), ahead(conv_gate_blk)),
            pl.BlockSpec((tm, D_MODEL), behind),
            const((1, D_ATT)), const((1, D_CONV)),
            pl.BlockSpec((None, D_CONV, D_CONV), lambda s: (layer, 0, 0), pipeline_mode=pl.Buffered(1)),
            pl.BlockSpec((None, D_MIX, D_MODEL), lambda s: (layer, 0, 0), pipeline_mode=pl.Buffered(1)),
        ] + [const((1, D_MODEL))] * emit_h,
        out_specs=[pl.BlockSpec((tm, D_MODEL), behind)] * (1 + emit_h),
        scratch_shapes=[pltpu.VMEM((tm, D_MIX), BF16), pltpu.VMEM((tm, D_MIX), BF16),
                        pltpu.VMEM((tm, D_CONV), F32)],
        compiler_params=pltpu.CompilerParams(
            dimension_semantics=("arbitrary",), vmem_limit_bytes=VMEM_LIMIT_BYTES),
        name="out_proj",
    )(att2, proj2, z2, proj2, x2, g, og, wpw_bf, w_bf, *([g_next] if emit_h else []))


def _rope_tables(seq):
    inv_freq = 1.0 / (ROPE_THETA ** (jnp.arange(0, HEAD_DIM, 2, dtype=F32) / HEAD_DIM))
    ang = jnp.arange(seq, dtype=F32)[:, None] * inv_freq[None, :]
    cos, sin = jnp.cos(ang), jnp.sin(ang)
    return jnp.concatenate([cos, cos], axis=-1), jnp.concatenate([-sin, sin], axis=-1)


def _mask_bias():
    rows = jnp.arange(Q_BLOCK)
    rows_mod4 = 4 * (rows % (Q_BLOCK // 4)) + rows // (Q_BLOCK // 4)
    ik = jnp.arange(2 * Q_BLOCK)[None, :]
    def bias(iq, shift):
        dist = iq[:, None] + shift - ik
        return jnp.where((dist >= 0) & (dist <= Q_BLOCK), 0.0, MASKED).astype(F32)
    return jnp.stack([bias(rows_mod4, 0), bias(rows_mod4, Q_BLOCK), bias(rows, 0), bias(rows, Q_BLOCK)])


def kernel(x, norm_g, w_in, q_norm_g, k_norm_g, dw_kernel, dw_bias, conv_ln_g, conv_ln_b, w_pw,
           att_out_g, conv_out_g, w_out):
    B, S, D = x.shape
    assert D == D_MODEL and S % (16 * 2 * Q_BLOCK) == 0
    depth = norm_g.shape[0]
    cos2, sin2 = _rope_tables(S)
    bias = _mask_bias()
    row = lambda v: v.reshape(1, -1)
    x2 = x.reshape(B * S, D)
    w_in_bf, w_pw_bf, w_out_bf = w_in.astype(BF16), w_pw.astype(BF16), w_out.astype(BF16)
    h2 = None
    for layer in range(depth):
        if h2 is None:
            proj2 = _norm_proj(x2, row(norm_g[layer]), w_in_bf, layer)
        else:
            proj2 = _proj(h2, w_in_bf, layer)
        proj3 = proj2.reshape(B, S, D_IN)
        att = _attention(proj3, cos2, sin2, row(q_norm_g[layer]), row(k_norm_g[layer]), bias)
        z = _conv_module(proj3, dw_kernel[layer], row(dw_bias[layer]), row(conv_ln_g[layer]),
                         row(conv_ln_b[layer]))
        g_next = row(norm_g[layer + 1]) if layer + 1 < depth else None
        outs = _out_proj(att.reshape(B * S, D_ATT), proj2, z.reshape(B * S, D_CONV), x2,
                         row(att_out_g[layer]), row(conv_out_g[layer]), w_pw_bf, w_out_bf, layer, g_next)
        x2, h2 = outs if g_next is not None else (outs[0], None)
    return x2.reshape(B, S, D)
```

```python
import functools

import jax
import jax.numpy as jnp
from jax import lax
from jax.experimental import pallas as pl
from jax.experimental.pallas import tpu as pltpu

D_MODEL = 2048
N_HEADS = 16
HEAD_DIM = 128
D_ATT = N_HEADS * HEAD_DIM
D_CONV = D_MODEL
D_MIX = D_ATT + D_CONV
D_IN = 4 * D_ATT + 3 * D_CONV
CONV_WIDTH = 31
DILATIONS = (1, 4, 16)
Q_BLOCK = 128
ROPE_THETA = 10000.0
EPS = 1e-6
MASKED = -1e30

SUBLANES = 8

VMEM_LIMIT_BYTES = 56 * 1024 * 1024

F32 = jnp.float32
BF16 = jnp.bfloat16


def _sigmoid(x):
    return 0.5 * jnp.tanh(0.5 * x) + 0.5


def _silu(x):
    h = 0.5 * x
    return h * jnp.tanh(h) + h


def _norm_proj_kernel(x_ref, g_ref, w_ref, o_ref, h_ref, *, row_chunk):
    tm = x_ref.shape[0]

    @pl.when(pl.program_id(1) == 0)
    def _():
        def body(i, carry):
            rows = pl.ds(pl.multiple_of(i * row_chunk, row_chunk), row_chunk)
            x = x_ref[rows, :]
            y = x * lax.rsqrt(jnp.mean(x * x, axis=-1, keepdims=True) + EPS)
            h_ref[rows, :] = (y * g_ref[...]).astype(BF16)
            return carry
        lax.fori_loop(0, tm // row_chunk, body, 0)

    o_ref[...] = jnp.dot(h_ref[...], w_ref[...], preferred_element_type=F32).astype(o_ref.dtype)


def _norm_proj(x2, g, w_bf, layer, *, tm=1024, tn=1024):
    T = x2.shape[0]
    return pl.pallas_call(
        functools.partial(_norm_proj_kernel, row_chunk=128),
        out_shape=jax.ShapeDtypeStruct((T, D_IN), BF16),
        grid=(T // tm, D_IN // tn),
        in_specs=[
            pl.BlockSpec((tm, D_MODEL), lambda i, j: (i, 0)),
            pl.BlockSpec((1, D_MODEL), lambda i, j: (0, 0)),
            pl.BlockSpec((None, D_MODEL, tn), lambda i, j: (layer, 0, j)),
        ],
        out_specs=pl.BlockSpec((tm, tn), lambda i, j: (i, j)),
        scratch_shapes=[pltpu.VMEM((tm, D_MODEL), BF16)],
        compiler_params=pltpu.CompilerParams(
            dimension_semantics=("arbitrary", "arbitrary"), vmem_limit_bytes=VMEM_LIMIT_BYTES),
        name="norm_proj",
    )(x2, g, w_bf)


def _proj_kernel(h_ref, w_ref, o_ref):
    o_ref[...] = jnp.dot(h_ref[...], w_ref[...], preferred_element_type=F32).astype(o_ref.dtype)


def _proj(h2, w_bf, layer, *, tm=2048, tn=1024):
    T = h2.shape[0]
    return pl.pallas_call(
        _proj_kernel,
        out_shape=jax.ShapeDtypeStruct((T, D_IN), BF16),
        grid=(T // tm, D_IN // tn),
        in_specs=[
            pl.BlockSpec((tm, D_MODEL), lambda i, j: (i, 0)),
            pl.BlockSpec((None, D_MODEL, tn), lambda i, j: (layer, 0, j)),
        ],
        out_specs=pl.BlockSpec((tm, tn), lambda i, j: (i, j)),
        compiler_params=pltpu.CompilerParams(
            dimension_semantics=("arbitrary", "arbitrary"), vmem_limit_bytes=VMEM_LIMIT_BYTES),
        name="proj",
    )(h2, w_bf)


NAT, MOD4 = 0, 1


def _attn_kernel(q_ref, k_ref, v_ref, cos_ref, sin_ref, gq_ref, gk_ref, bias_ref, o_ref,
                 qs, ks, vs, t0, q4, k4, v4, *stats, seq, group):
    obuf, lbuf, mbuf = stats[0:3], stats[3:6], stats[6:9]
    chunk = 256
    quarter = seq // 4

    @pl.when((pl.program_id(0) == 0) & (pl.program_id(1) == 0))
    def _():
        for layout in range(2):
            vs[layout, :, HEAD_DIM:2 * HEAD_DIM] = jnp.ones((seq, HEAD_DIM), BF16)

    def prepare(src_ref, g_ref, dst, mod4_f32, first_layout, normalize, scale):
        if normalize:
            gain = g_ref[...] * scale
            gain_rot = pltpu.roll(gain, HEAD_DIM // 2, 1)

        def body(i, carry):
            rows = pl.ds(pl.multiple_of(i * chunk, chunk), chunk)
            raw = src_ref[rows, :]
            y = raw.astype(F32)
            if normalize:
                rot = pltpu.roll(raw, HEAD_DIM // 2, 1).astype(F32)
                inv = lax.rsqrt(jnp.mean(y * y, axis=-1, keepdims=True) + EPS)
                y = (y * gain * cos_ref[rows, :] + rot * gain_rot * sin_ref[rows, :]) * inv
            t0[rows, :] = y
            if first_layout == NAT:
                dst[NAT, rows, 0:HEAD_DIM] = y.astype(BF16)
            return carry
        lax.fori_loop(0, seq // chunk, body, 0, unroll=8)

        for r4 in range(4):
            def body4(i, carry, r4=r4):
                y = t0[pl.ds(r4 + i * (4 * chunk), chunk, stride=4), :]
                rows = pl.ds(pl.multiple_of(r4 * quarter + i * chunk, chunk), chunk)
                mod4_f32[rows, :] = y
                dst[MOD4 - first_layout, rows, 0:HEAD_DIM] = y.astype(BF16)
                return carry
            lax.fori_loop(0, quarter // chunk, body4, 0)

    prepare(q_ref, gq_ref, qs, q4, MOD4, True, HEAD_DIM ** -0.5)
    prepare(k_ref, gk_ref, ks, k4, NAT, True, 1.0)
    prepare(v_ref, None, vs, v4, NAT, False, 1.0)

    n_blocks = seq // Q_BLOCK
    piece = Q_BLOCK // 4
    for br, d in enumerate(DILATIONS):
        nb = n_blocks // d

        def one_block(c, br=br, d=d, nb=nb):
            n = lax.rem(c, nb)
            first = n == 0
            q_row = pl.multiple_of(c * Q_BLOCK, Q_BLOCK)
            k_row = pl.multiple_of(jnp.where(first, c * Q_BLOCK, (c - 1) * Q_BLOCK), Q_BLOCK)
            if d == 1:
                runs = [pl.ds(pl.multiple_of(r4 * quarter + c * piece, piece), piece) for r4 in range(4)]
                qb = jnp.concatenate([qs[0, run, :] for run in runs], axis=0)
            elif d == 4:
                qb = qs[0, pl.ds(q_row, Q_BLOCK), :]
            if d == 16:
                r = c // nb
                base = (r % 4) * quarter + r // 4
                qb = q4[pl.ds(base + n * (4 * Q_BLOCK), Q_BLOCK, stride=4), :].astype(BF16)
                keys = pl.ds(base, 2 * Q_BLOCK, stride=4)
                kb = k4[keys, :].astype(BF16)
                vb = jnp.concatenate([v4[keys, :].astype(BF16), jnp.ones((2 * Q_BLOCK, HEAD_DIM), BF16)], axis=1)
            else:
                kb = ks[br, pl.ds(k_row, 2 * Q_BLOCK), :]
                vb = vs[br, pl.ds(k_row, 2 * Q_BLOCK), :]
            s = lax.dot_general(qb, kb, (((1,), (1,)), ((), ())), preferred_element_type=F32)
            s = s + bias_ref[jnp.where(first, 0, 1) + (0 if d == 1 else 2)]
            m = jnp.max(s, axis=-1, keepdims=True)
            p = jnp.exp(s - m).astype(BF16)
            oe = jnp.dot(p, vb, preferred_element_type=F32)
            mb = jnp.broadcast_to(m, (Q_BLOCK, HEAD_DIM))
            ob, lb = oe[:, 0:HEAD_DIM], oe[:, HEAD_DIM:2 * HEAD_DIM]
            if d == 1:
                for r4, run in enumerate(runs):
                    sub = slice(r4 * piece, (r4 + 1) * piece)
                    obuf[br][run, :] = ob[sub, :]
                    lbuf[br][run, :] = lb[sub, :]
                    mbuf[br][run, :] = mb[sub, :]
            else:
                if d == 4:
                    rows = pl.ds(q_row, Q_BLOCK)
                else:
                    rows = pl.ds(base + n * (4 * Q_BLOCK), Q_BLOCK, stride=4)
                obuf[br][rows, :] = ob
                lbuf[br][rows, :] = lb
                mbuf[br][rows, :] = mb

        def block_group(it, carry, one_block=one_block):
            for g in range(group):
                one_block(it * group + g)
            return carry
        lax.fori_loop(0, n_blocks // group, block_group, 0)

    for r4 in range(4):
        def finish(i, carry, r4=r4):
            rows = pl.ds(pl.multiple_of(r4 * quarter + i * chunk, chunk), chunk)
            ms = [mbuf[br][rows, :] for br in range(3)]
            m_all = jnp.maximum(jnp.maximum(ms[0], ms[1]), ms[2])
            num = den = None
            for br in range(3):
                w = jnp.exp(ms[br] - m_all)
                t_num = w * obuf[br][rows, :]
                t_den = w * lbuf[br][rows, :]
                num = t_num if num is None else num + t_num
                den = t_den if den is None else den + t_den
            o_ref[pl.ds(r4 + i * (4 * chunk), chunk, stride=4), :] = (num / den).astype(o_ref.dtype)
            return carry
        lax.fori_loop(0, quarter // chunk, finish, 0, unroll=2)


def _attention(proj3, cos2, sin2, gq, gk, bias, *, group=32):
    B, S, _ = proj3.shape
    head_spec = lambda off: pl.BlockSpec((None, S, HEAD_DIM), lambda b, h: (b, 0, off + h))
    const = lambda shape: pl.BlockSpec(shape, lambda b, h: (0,) * len(shape), pipeline_mode=pl.Buffered(1))
    return pl.pallas_call(
        functools.partial(_attn_kernel, seq=S, group=group),
        out_shape=jax.ShapeDtypeStruct((B, S, D_ATT), F32),
        grid=(B, N_HEADS),
        in_specs=[
            head_spec(0), head_spec(N_HEADS), head_spec(2 * N_HEADS),
            const((S, HEAD_DIM)), const((S, HEAD_DIM)),
            const((1, HEAD_DIM)), const((1, HEAD_DIM)),
            const((4, Q_BLOCK, 2 * Q_BLOCK)),
        ],
        out_specs=pl.BlockSpec((None, S, HEAD_DIM), lambda b, h: (b, 0, h)),
        scratch_shapes=[
            pltpu.VMEM((1, S, HEAD_DIM), BF16), pltpu.VMEM((2, S, HEAD_DIM), BF16),
            pltpu.VMEM((2, S, 2 * HEAD_DIM), BF16),
        ] + [pltpu.VMEM((S, HEAD_DIM), F32)] * (4 + 9),
        compiler_params=pltpu.CompilerParams(
            dimension_semantics=("arbitrary", "arbitrary"), vmem_limit_bytes=VMEM_LIMIT_BYTES),
        name="dilated_attention",
    )(proj3, proj3, proj3, cos2, sin2, gq, gk, bias)


CONV_HALO = 32


def _conv_kernel(a_ref, b_ref, dw_ref, dwb_ref, lng_ref, lnb_ref, o_ref, u_ext, y_s, *, row_chunk, conv_rows):
    tm = a_ref.shape[0]

    def row_loop(body, n_rows):
        def step(i, carry):
            body(pl.ds(pl.multiple_of(i * n_rows, n_rows), n_rows),
                 pl.ds(pl.multiple_of(i * n_rows + CONV_HALO, SUBLANES), n_rows))
            return carry
        lax.fori_loop(0, tm // n_rows, step, 0)

    @pl.when(pl.program_id(1) == 0)
    def _():
        u_ext[0:CONV_HALO, :] = jnp.zeros((CONV_HALO, D_CONV), F32)

    def glu(rows, ext_rows):
        u_ext[ext_rows, :] = a_ref[rows, :].astype(F32) * _sigmoid(b_ref[rows, :].astype(F32))
    row_loop(glu, 2 * row_chunk)

    lead = CONV_HALO - (CONV_WIDTH - 1)

    def lanes_body(cc, carry):
        lanes = pl.ds(pl.multiple_of(cc * 128, 128), 128)
        for rc in range(tm // conv_rows):
            r0 = rc * conv_rows
            acc = jnp.broadcast_to(dwb_ref[:, lanes], (conv_rows, 128))
            for shift in range(SUBLANES):
                n_rows = conv_rows + (SUBLANES if shift else 0)
                part = None
                for off in range(shift, lead + CONV_WIDTH, SUBLANES):
                    if off < lead:
                        continue
                    start = r0 + off - shift
                    term = dw_ref[off - lead:off - lead + 1, lanes] * u_ext[start:start + n_rows, lanes]
                    part = term if part is None else part + term
                acc = acc + part[shift:shift + conv_rows, :]
            y_s[r0:r0 + conv_rows, lanes] = acc
        return carry
    lax.fori_loop(0, D_CONV // 128, lanes_body, 0)

    u_ext[0:CONV_HALO, :] = u_ext[tm:tm + CONV_HALO, :]

    def layer_norm_swish(rows, ext_rows):
        y = y_s[rows, :]
        yc = y - jnp.mean(y, axis=-1, keepdims=True)
        yn = yc * lax.rsqrt(jnp.mean(yc * yc, axis=-1, keepdims=True) + EPS)
        o_ref[rows, :] = _silu(yn * lng_ref[...] + lnb_ref[...]).astype(o_ref.dtype)
    row_loop(layer_norm_swish, 2 * row_chunk)


def _conv_module(proj3, dw, dwb, lng, lnb, *, tm=512):
    B, S, _ = proj3.shape
    col = lambda blk: pl.BlockSpec((None, tm, D_CONV), lambda b, i: (b, i, blk))
    const = lambda shape: pl.BlockSpec(shape, lambda b, i: (0,) * len(shape), pipeline_mode=pl.Buffered(1))
    base = 4 * D_ATT // D_CONV
    return pl.pallas_call(
        functools.partial(_conv_kernel, row_chunk=64, conv_rows=128),
        out_shape=jax.ShapeDtypeStruct((B, S, D_CONV), BF16),
        grid=(B, S // tm),
        in_specs=[
            col(base), col(base + 1),
            const((CONV_WIDTH, D_CONV)), const((1, D_CONV)), const((1, D_CONV)), const((1, D_CONV)),
        ],
        out_specs=pl.BlockSpec((None, tm, D_CONV), lambda b, i: (b, i, 0)),
        scratch_shapes=[pltpu.VMEM((tm + CONV_HALO, D_CONV), F32), pltpu.VMEM((tm, D_CONV), F32)],
        compiler_params=pltpu.CompilerParams(
            dimension_semantics=("arbitrary", "arbitrary"), vmem_limit_bytes=VMEM_LIMIT_BYTES),
        name="conformer_conv",
    )(proj3, proj3, dw, dwb, lng, lnb)


def _out_kernel(att_ref, ga_ref, z_ref, gc_ref, x_ref, g_ref, og_ref, wpw_ref, w_ref, *rest,
                row_chunk, n_tiles, emit_h):
    if emit_h:
        gn_ref, o_ref, h_ref, lhs_a, lhs_b, y_pw = rest
    else:
        o_ref, lhs_a, lhs_b, y_pw = rest
    s = pl.program_id(0)
    tm = att_ref.shape[0]

    def gate_att_rows(dst, rows):
        a = att_ref[rows, :]
        an = a * lax.rsqrt(jnp.mean(a * a, axis=-1, keepdims=True) + EPS) * g_ref[...]
        dst[rows, 0:D_ATT] = (an * _silu(ga_ref[rows, :].astype(F32))).astype(BF16)

    def pointwise():
        y_pw[...] = jnp.dot(z_ref[...], wpw_ref[...], preferred_element_type=F32)

    def gate_conv_rows(dst, rows):
        c = y_pw[rows, :]
        cn = c * lax.rsqrt(jnp.mean(c * c, axis=-1, keepdims=True) + EPS) * og_ref[...]
        dst[rows, D_ATT:D_MIX] = (cn * _silu(gc_ref[rows, :].astype(F32))).astype(BF16)

    def project(src):
        o_ref[...] = x_ref[...] + jnp.dot(src[...], w_ref[...], preferred_element_type=F32)
        if emit_h:
            for i in range(tm // row_chunk):
                rows = pl.ds(i * row_chunk, row_chunk)
                xn = o_ref[rows, :]
                hn = xn * lax.rsqrt(jnp.mean(xn * xn, axis=-1, keepdims=True) + EPS)
                h_ref[rows, :] = (hn * gn_ref[...]).astype(BF16)

    @pl.when(s == 0)
    def _():
        pointwise()

        def body(i, carry):
            rows = pl.ds(pl.multiple_of(i * row_chunk, row_chunk), row_chunk)
            gate_att_rows(lhs_a, rows)
            gate_conv_rows(lhs_a, rows)
            return carry
        lax.fori_loop(0, tm // row_chunk, body, 0)

    def steady(src, dst):
        pointwise()
        project(src)
        for i in range(tm // row_chunk):
            gate_att_rows(dst, pl.ds(i * row_chunk, row_chunk))
        for i in range(tm // row_chunk):
            gate_conv_rows(dst, pl.ds(i * row_chunk, row_chunk))

    @pl.when((s > 0) & (s < n_tiles) & (lax.rem(s, 2) == 1))
    def _():
        steady(lhs_a, lhs_b)

    @pl.when((s > 0) & (s < n_tiles) & (lax.rem(s, 2) == 0))
    def _():
        steady(lhs_b, lhs_a)

    @pl.when(s == n_tiles)
    def _():
        project(lhs_a if n_tiles % 2 == 1 else lhs_b)


def _out_proj(att2, proj2, z2, x2, g, og, wpw_bf, w_bf, layer, g_next=None, *, tm=256):
    T = x2.shape[0]
    emit_h = g_next is not None
    n_tiles = T // tm
    gate_blk = 3 * D_ATT // D_ATT
    conv_gate_blk = (4 * D_ATT + 2 * D_CONV) // D_CONV
    const = lambda shape: pl.BlockSpec(shape, lambda s: (0,) * len(shape), pipeline_mode=pl.Buffered(1))
    ahead = lambda blk: (lambda s: (jnp.minimum(s, n_tiles - 1), blk))
    behind = lambda s: (jnp.maximum(s - 1, 0), 0)
    return pl.pallas_call(
        functools.partial(_out_kernel, row_chunk=16, n_tiles=n_tiles, emit_h=emit_h),
        out_shape=[jax.ShapeDtypeStruct((T, D_MODEL), F32)] + [jax.ShapeDtypeStruct((T, D_MODEL), BF16)] * emit_h,
        grid=(n_tiles + 1,),
        in_specs=[
            pl.BlockSpec((tm, D_ATT), ahead(0)),
            pl.BlockSpec((tm, D_ATT), ahead(gate_blk)),
            pl.BlockSpec((tm, D_CONV), ahead(0)),
            pl.BlockSpec((tm, D_CONV), ahead(conv_gate_blk)),
            pl.BlockSpec((tm, D_MODEL), behind),
            const((1, D_ATT)), const((1, D_CONV)),
            pl.BlockSpec((None, D_CONV, D_CONV), lambda s: (layer, 0, 0), pipeline_mode=pl.Buffered(1)),
            pl.BlockSpec((None, D_MIX, D_MODEL), lambda s: (layer, 0, 0), pipeline_mode=pl.Buffered(1)),
        ] + [const((1, D_MODEL))] * emit_h,
        out_specs=[pl.BlockSpec((tm, D_MODEL), behind)] * (1 + emit_h),
        scratch_shapes=[pltpu.VMEM((tm, D_MIX), BF16), pltpu.VMEM((tm, D_MIX), BF16),
                        pltpu.VMEM((tm, D_CONV), F32)],
        compiler_params=pltpu.CompilerParams(
            dimension_semantics=("arbitrary",), vmem_limit_bytes=VMEM_LIMIT_BYTES),
        name="out_proj",
    )(att2, proj2, z2, proj2, x2, g, og, wpw_bf, w_bf, *([g_next] if emit_h else []))


def _rope_tables(seq):
    inv_freq = 1.0 / (ROPE_THETA ** (jnp.arange(0, HEAD_DIM, 2, dtype=F32) / HEAD_DIM))
    ang = jnp.arange(seq, dtype=F32)[:, None] * inv_freq[None, :]
    cos, sin = jnp.cos(ang), jnp.sin(ang)
    return jnp.concatenate([cos, cos], axis=-1), jnp.concatenate([-sin, sin], axis=-1)


def _mask_bias():
    rows = jnp.arange(Q_BLOCK)
    rows_mod4 = 4 * (rows % (Q_BLOCK // 4)) + rows // (Q_BLOCK // 4)
    ik = jnp.arange(2 * Q_BLOCK)[None, :]
    def bias(iq, shift):
        dist = iq[:, None] + shift - ik
        return jnp.where((dist >= 0) & (dist <= Q_BLOCK), 0.0, MASKED).astype(F32)
    return jnp.stack([bias(rows_mod4, 0), bias(rows_mod4, Q_BLOCK), bias(rows, 0), bias(rows, Q_BLOCK)])


def kernel(x, norm_g, w_in, q_norm_g, k_norm_g, dw_kernel, dw_bias, conv_ln_g, conv_ln_b, w_pw,
           att_out_g, conv_out_g, w_out):
    B, S, D = x.shape
    assert D == D_MODEL and S % (16 * 2 * Q_BLOCK) == 0
    depth = norm_g.shape[0]
    cos2, sin2 = _rope_tables(S)
    bias = _mask_bias()
    row = lambda v: v.reshape(1, -1)
    x2 = x.reshape(B * S, D)
    w_in_bf, w_pw_bf, w_out_bf = w_in.astype(BF16), w_pw.astype(BF16), w_out.astype(BF16)
    h2 = None
    for layer in range(depth):
        if h2 is None:
            proj2 = _norm_proj(x2, row(norm_g[layer]), w_in_bf, layer)
        else:
            proj2 = _proj(h2, w_in_bf, layer)
        proj3 = proj2.reshape(B, S, D_IN)
        att = _attention(proj3, cos2, sin2, row(q_norm_g[layer]), row(k_norm_g[layer]), bias)
        z = _conv_module(proj3, dw_kernel[layer], row(dw_bias[layer]), row(conv_ln_g[layer]),
                         row(conv_ln_b[layer]))
        g_next = row(norm_g[layer + 1]) if layer + 1 < depth else None
        outs = _out_proj(att.reshape(B * S, D_ATT), proj2, z.reshape(B * S, D_CONV), x2,
                         row(att_out_g[layer]), row(conv_out_g[layer]), w_pw_bf, w_out_bf, layer, g_next)
        x2, h2 = outs if g_next is not None else (outs[0], None)
    return x2.reshape(B, S, D)
```

```python
import functools

import jax
import jax.numpy as jnp
from jax import lax
from jax.experimental import pallas as pl
from jax.experimental.pallas import tpu as pltpu

D_MODEL = 2048
N_HEADS = 16
HEAD_DIM = 128
D_ATT = N_HEADS * HEAD_DIM
D_CONV = D_MODEL
D_MIX = D_ATT + D_CONV
D_IN = 4 * D_ATT + 3 * D_CONV
CONV_WIDTH = 31
DILATIONS = (1, 4, 16)
Q_BLOCK = 128
ROPE_THETA = 10000.0
EPS = 1e-6
MASKED = -1e30

SUBLANES = 8

VMEM_LIMIT_BYTES = 56 * 1024 * 1024

F32 = jnp.float32
BF16 = jnp.bfloat16


def _sigmoid(x):
    return 0.5 * jnp.tanh(0.5 * x) + 0.5


def _silu(x):
    h = 0.5 * x
    return h * jnp.tanh(h) + h


def _norm_proj_kernel(x_ref, g_ref, w_ref, o_ref, h_ref, *, row_chunk):
    tm = x_ref.shape[0]

    @pl.when(pl.program_id(1) == 0)
    def _():
        def body(i, carry):
            rows = pl.ds(pl.multiple_of(i * row_chunk, row_chunk), row_chunk)
            x = x_ref[rows, :]
            y = x * lax.rsqrt(jnp.mean(x * x, axis=-1, keepdims=True) + EPS)
            h_ref[rows, :] = (y * g_ref[...]).astype(BF16)
            return carry
        lax.fori_loop(0, tm // row_chunk, body, 0)

    o_ref[...] = jnp.dot(h_ref[...], w_ref[...], preferred_element_type=F32).astype(o_ref.dtype)


def _norm_proj(x2, g, w_bf, layer, *, tm=1024, tn=1024):
    T = x2.shape[0]
    return pl.pallas_call(
        functools.partial(_norm_proj_kernel, row_chunk=128),
        out_shape=jax.ShapeDtypeStruct((T, D_IN), BF16),
        grid=(T // tm, D_IN // tn),
        in_specs=[
            pl.BlockSpec((tm, D_MODEL), lambda i, j: (i, 0)),
            pl.BlockSpec((1, D_MODEL), lambda i, j: (0, 0)),
            pl.BlockSpec((None, D_MODEL, tn), lambda i, j: (layer, 0, j)),
        ],
        out_specs=pl.BlockSpec((tm, tn), lambda i, j: (i, j)),
        scratch_shapes=[pltpu.VMEM((tm, D_MODEL), BF16)],
        compiler_params=pltpu.CompilerParams(
            dimension_semantics=("arbitrary", "arbitrary"), vmem_limit_bytes=VMEM_LIMIT_BYTES),
        name="norm_proj",
    )(x2, g, w_bf)


def _proj_kernel(h_ref, w_ref, o_ref):
    o_ref[...] = jnp.dot(h_ref[...], w_ref[...], preferred_element_type=F32).astype(o_ref.dtype)


def _proj(h2, w_bf, layer, *, tm=2048, tn=1024):
    T = h2.shape[0]
    return pl.pallas_call(
        _proj_kernel,
        out_shape=jax.ShapeDtypeStruct((T, D_IN), BF16),
        grid=(T // tm, D_IN // tn),
        in_specs=[
            pl.BlockSpec((tm, D_MODEL), lambda i, j: (i, 0)),
            pl.BlockSpec((None, D_MODEL, tn), lambda i, j: (layer, 0, j)),
        ],
        out_specs=pl.BlockSpec((tm, tn), lambda i, j: (i, j)),
        compiler_params=pltpu.CompilerParams(
            dimension_semantics=("arbitrary", "arbitrary"), vmem_limit_bytes=VMEM_LIMIT_BYTES),
        name="proj",
    )(h2, w_bf)


NAT, MOD4 = 0, 1


def _attn_kernel(q_ref, k_ref, v_ref, cos_ref, sin_ref, gq_ref, gk_ref, bias_ref, perm_ref, o_ref,
                 qs, ks, vs, t0, q4, k4, v4, *stats, seq, group):
    obuf, lbuf, mbuf = stats[0:3], stats[3:6], stats[6:9]
    chunk = 256
    quarter = seq // 4

    @pl.when((pl.program_id(0) == 0) & (pl.program_id(1) == 0))
    def _():
        for layout in range(2):
            vs[layout, :, HEAD_DIM:2 * HEAD_DIM] = jnp.ones((seq, HEAD_DIM), BF16)

    def prepare(src_ref, g_ref, dst, mod4_f32, first_layout, normalize, scale):
        if normalize:
            gain = g_ref[...] * scale
            gain_rot = pltpu.roll(gain, HEAD_DIM // 2, 1)

        def body(i, carry):
            rows = pl.ds(pl.multiple_of(i * chunk, chunk), chunk)
            raw = src_ref[rows, :]
            y = raw.astype(F32)
            if normalize:
                rot = jnp.dot(raw, perm_ref[...], preferred_element_type=F32)
                inv = lax.rsqrt(jnp.mean(y * y, axis=-1, keepdims=True) + EPS)
                y = (y * gain * cos_ref[rows, :] + rot * gain_rot * sin_ref[rows, :]) * inv
            t0[rows, :] = y
            if first_layout == NAT:
                dst[NAT, rows, 0:HEAD_DIM] = y.astype(BF16)
            return carry
        lax.fori_loop(0, seq // chunk, body, 0, unroll=8)

        for r4 in range(4):
            def body4(i, carry, r4=r4):
                y = t0[pl.ds(r4 + i * (4 * chunk), chunk, stride=4), :]
                rows = pl.ds(pl.multiple_of(r4 * quarter + i * chunk, chunk), chunk)
                mod4_f32[rows, :] = y
                dst[MOD4 - first_layout, rows, 0:HEAD_DIM] = y.astype(BF16)
                return carry
            lax.fori_loop(0, quarter // chunk, body4, 0)

    prepare(q_ref, gq_ref, qs, q4, MOD4, True, HEAD_DIM ** -0.5)
    prepare(k_ref, gk_ref, ks, k4, NAT, True, 1.0)
    prepare(v_ref, None, vs, v4, NAT, False, 1.0)

    n_blocks = seq // Q_BLOCK
    piece = Q_BLOCK // 4
    for br, d in enumerate(DILATIONS):
        nb = n_blocks // d

        def one_block(c, br=br, d=d, nb=nb):
            n = lax.rem(c, nb)
            first = n == 0
            q_row = pl.multiple_of(c * Q_BLOCK, Q_BLOCK)
            k_row = pl.multiple_of(jnp.where(first, c * Q_BLOCK, (c - 1) * Q_BLOCK), Q_BLOCK)
            if d == 1:
                runs = [pl.ds(pl.multiple_of(r4 * quarter + c * piece, piece), piece) for r4 in range(4)]
                qb = jnp.concatenate([qs[0, run, :] for run in runs], axis=0)
            elif d == 4:
                qb = qs[0, pl.ds(q_row, Q_BLOCK), :]
            if d == 16:
                r = c // nb
                base = (r % 4) * quarter + r // 4
                qb = q4[pl.ds(base + n * (4 * Q_BLOCK), Q_BLOCK, stride=4), :].astype(BF16)
                keys = pl.ds(base, 2 * Q_BLOCK, stride=4)
                kb = k4[keys, :].astype(BF16)
                vb = jnp.concatenate([v4[keys, :].astype(BF16), jnp.ones((2 * Q_BLOCK, HEAD_DIM), BF16)], axis=1)
            else:
                kb = ks[br, pl.ds(k_row, 2 * Q_BLOCK), :]
                vb = vs[br, pl.ds(k_row, 2 * Q_BLOCK), :]
            s = lax.dot_general(qb, kb, (((1,), (1,)), ((), ())), preferred_element_type=F32)
            s = s + bias_ref[jnp.where(first, 0, 1) + (0 if d == 1 else 2)]
            m = jnp.max(s, axis=-1, keepdims=True)
            p = jnp.exp(s - m).astype(BF16)
            oe = jnp.dot(p, vb, preferred_element_type=F32)
            mb = jnp.broadcast_to(m, (Q_BLOCK, HEAD_DIM))
            ob, lb = oe[:, 0:HEAD_DIM], oe[:, HEAD_DIM:2 * HEAD_DIM]
            if d == 1:
                for r4, run in enumerate(runs):
                    sub = slice(r4 * piece, (r4 + 1) * piece)
                    obuf[br][run, :] = ob[sub, :]
                    lbuf[br][run, :] = lb[sub, :]
                    mbuf[br][run, :] = mb[sub, :]
            else:
                if d == 4:
                    rows = pl.ds(q_row, Q_BLOCK)
                else:
                    rows = pl.ds(base + n * (4 * Q_BLOCK), Q_BLOCK, stride=4)
                obuf[br][rows, :] = ob
                lbuf[br][rows, :] = lb
                mbuf[br][rows, :] = mb

        def block_group(it, carry, one_block=one_block):
            for g in range(group):
                one_block(it * group + g)
            return carry
        lax.fori_loop(0, n_blocks // group, block_group, 0)

    for r4 in range(4):
        def finish(i, carry, r4=r4):
            rows = pl.ds(pl.multiple_of(r4 * quarter + i * chunk, chunk), chunk)
            ms = [mbuf[br][rows, :] for br in range(3)]
            m_all = jnp.maximum(jnp.maximum(ms[0], ms[1]), ms[2])
            num = den = None
            for br in range(3):
                w = jnp.exp(ms[br] - m_all)
                t_num = w * obuf[br][rows, :]
                t_den = w * lbuf[br][rows, :]
                num = t_num if num is None else num + t_num
                den = t_den if den is None else den + t_den
            o_ref[pl.ds(r4 + i * (4 * chunk), chunk, stride=4), :] = (num / den).astype(o_ref.dtype)
            return carry
        lax.fori_loop(0, quarter // chunk, finish, 0, unroll=2)


def _attention(proj3, cos2, sin2, gq, gk, bias, *, group=32):
    B, S, _ = proj3.shape
    head_spec = lambda off: pl.BlockSpec((None, S, HEAD_DIM), lambda b, h: (b, 0, off + h))
    const = lambda shape: pl.BlockSpec(shape, lambda b, h: (0,) * len(shape), pipeline_mode=pl.Buffered(1))
    return pl.pallas_call(
        functools.partial(_attn_kernel, seq=S, group=group),
        out_shape=jax.ShapeDtypeStruct((B, S, D_ATT), F32),
        grid=(B, N_HEADS),
        in_specs=[
            head_spec(0), head_spec(N_HEADS), head_spec(2 * N_HEADS),
            const((S, HEAD_DIM)), const((S, HEAD_DIM)),
            const((1, HEAD_DIM)), const((1, HEAD_DIM)),
            const((4, Q_BLOCK, 2 * Q_BLOCK)),
            const((HEAD_DIM, HEAD_DIM)),
        ],
        out_specs=pl.BlockSpec((None, S, HEAD_DIM), lambda b, h: (b, 0, h)),
        scratch_shapes=[
            pltpu.VMEM((1, S, HEAD_DIM), BF16), pltpu.VMEM((2, S, HEAD_DIM), BF16),
            pltpu.VMEM((2, S, 2 * HEAD_DIM), BF16),
        ] + [pltpu.VMEM((S, HEAD_DIM), F32)] * (4 + 9),
        compiler_params=pltpu.CompilerParams(
            dimension_semantics=("arbitrary", "arbitrary"), vmem_limit_bytes=VMEM_LIMIT_BYTES),
        name="dilated_attention",
    )(proj3, proj3, proj3, cos2, sin2, gq, gk, bias,
      jnp.roll(jnp.eye(HEAD_DIM, dtype=BF16), HEAD_DIM // 2, axis=1))


CONV_HALO = 32


def _conv_kernel(a_ref, b_ref, dw_ref, dwb_ref, lng_ref, lnb_ref, o_ref, u_ext, y_s, *, row_chunk, conv_rows):
    tm = a_ref.shape[0]

    def row_loop(body, n_rows):
        def step(i, carry):
            body(pl.ds(pl.multiple_of(i * n_rows, n_rows), n_rows),
                 pl.ds(pl.multiple_of(i * n_rows + CONV_HALO, SUBLANES), n_rows))
            return carry
        lax.fori_loop(0, tm // n_rows, step, 0)

    @pl.when(pl.program_id(1) == 0)
    def _():
        u_ext[0:CONV_HALO, :] = jnp.zeros((CONV_HALO, D_CONV), F32)

    def glu(rows, ext_rows):
        u_ext[ext_rows, :] = a_ref[rows, :].astype(F32) * _sigmoid(b_ref[rows, :].astype(F32))
    row_loop(glu, 2 * row_chunk)

    lead = CONV_HALO - (CONV_WIDTH - 1)

    def lanes_body(cc, carry):
        lanes = pl.ds(pl.multiple_of(cc * 128, 128), 128)
        for rc in range(tm // conv_rows):
            r0 = rc * conv_rows
            acc = jnp.broadcast_to(dwb_ref[:, lanes], (conv_rows, 128))
            for shift in range(SUBLANES):
                n_rows = conv_rows + (SUBLANES if shift else 0)
                part = None
                for off in range(shift, lead + CONV_WIDTH, SUBLANES):
                    if off < lead:
                        continue
                    start = r0 + off - shift
                    term = dw_ref[off - lead:off - lead + 1, lanes] * u_ext[start:start + n_rows, lanes]
                    part = term if part is None else part + term
                acc = acc + part[shift:shift + conv_rows, :]
            y_s[r0:r0 + conv_rows, lanes] = acc
        return carry
    lax.fori_loop(0, D_CONV // 128, lanes_body, 0)

    u_ext[0:CONV_HALO, :] = u_ext[tm:tm + CONV_HALO, :]

    def layer_norm_swish(rows, ext_rows):
        y = y_s[rows, :]
        yc = y - jnp.mean(y, axis=-1, keepdims=True)
        yn = yc * lax.rsqrt(jnp.mean(yc * yc, axis=-1, keepdims=True) + EPS)
        o_ref[rows, :] = _silu(yn * lng_ref[...] + lnb_ref[...]).astype(o_ref.dtype)
    row_loop(layer_norm_swish, 2 * row_chunk)


def _conv_module(proj3, dw, dwb, lng, lnb, *, tm=512):
    B, S, _ = proj3.shape
    col = lambda blk: pl.BlockSpec((None, tm, D_CONV), lambda b, i: (b, i, blk))
    const = lambda shape: pl.BlockSpec(shape, lambda b, i: (0,) * len(shape), pipeline_mode=pl.Buffered(1))
    base = 4 * D_ATT // D_CONV
    return pl.pallas_call(
        functools.partial(_conv_kernel, row_chunk=64, conv_rows=128),
        out_shape=jax.ShapeDtypeStruct((B, S, D_CONV), BF16),
        grid=(B, S // tm),
        in_specs=[
            col(base), col(base + 1),
            const((CONV_WIDTH, D_CONV)), const((1, D_CONV)), const((1, D_CONV)), const((1, D_CONV)),
        ],
        out_specs=pl.BlockSpec((None, tm, D_CONV), lambda b, i: (b, i, 0)),
        scratch_shapes=[pltpu.VMEM((tm + CONV_HALO, D_CONV), F32), pltpu.VMEM((tm, D_CONV), F32)],
        compiler_params=pltpu.CompilerParams(
            dimension_semantics=("arbitrary", "arbitrary"), vmem_limit_bytes=VMEM_LIMIT_BYTES),
        name="conformer_conv",
    )(proj3, proj3, dw, dwb, lng, lnb)


def _out_kernel(att_ref, ga_ref, z_ref, gc_ref, x_ref, g_ref, og_ref, wpw_ref, w_ref, *rest,
                row_chunk, n_tiles, emit_h):
    if emit_h:
        gn_ref, o_ref, h_ref, lhs_a, lhs_b, y_pw = rest
    else:
        o_ref, lhs_a, lhs_b, y_pw = rest
    s = pl.program_id(0)
    tm = att_ref.shape[0]

    def gate_att_rows(dst, rows):
        a = att_ref[rows, :]
        an = a * lax.rsqrt(jnp.mean(a * a, axis=-1, keepdims=True) + EPS) * g_ref[...]
        dst[rows, 0:D_ATT] = (an * _silu(ga_ref[rows, :].astype(F32))).astype(BF16)

    def pointwise():
        y_pw[...] = jnp.dot(z_ref[...], wpw_ref[...], preferred_element_type=F32)

    def gate_conv_rows(dst, rows):
        c = y_pw[rows, :]
        cn = c * lax.rsqrt(jnp.mean(c * c, axis=-1, keepdims=True) + EPS) * og_ref[...]
        dst[rows, D_ATT:D_MIX] = (cn * _silu(gc_ref[rows, :].astype(F32))).astype(BF16)

    def project(src):
        o_ref[...] = x_ref[...] + jnp.dot(src[...], w_ref[...], preferred_element_type=F32)
        if emit_h:
            for i in range(tm // row_chunk):
                rows = pl.ds(i * row_chunk, row_chunk)
                xn = o_ref[rows, :]
                hn = xn * lax.rsqrt(jnp.mean(xn * xn, axis=-1, keepdims=True) + EPS)
                h_ref[rows, :] = (hn * gn_ref[...]).astype(BF16)

    @pl.when(s == 0)
    def _():
        pointwise()

        def body(i, carry):
            rows = pl.ds(pl.multiple_of(i * row_chunk, row_chunk), row_chunk)
            gate_att_rows(lhs_a, rows)
            gate_conv_rows(lhs_a, rows)
            return carry
        lax.fori_loop(0, tm // row_chunk, body, 0)

    def steady(src, dst):
        pointwise()
        project(src)
        for i in range(tm // row_chunk):
            gate_att_rows(dst, pl.ds(i * row_chunk, row_chunk))
        for i in range(tm // row_chunk):
            gate_conv_rows(dst, pl.ds(i * row_chunk, row_chunk))

    @pl.when((s > 0) & (s < n_tiles) & (lax.rem(s, 2) == 1))
    def _():
        steady(lhs_a, lhs_b)

    @pl.when((s > 0) & (s < n_tiles) & (lax.rem(s, 2) == 0))
    def _():
        steady(lhs_b, lhs_a)

    @pl.when(s == n_tiles)
    def _():
        project(lhs_a if n_tiles % 2 == 1 else lhs_b)


def _out_proj(att2, proj2, z2, x2, g, og, wpw_bf, w_bf, layer, g_next=None, *, tm=256):
    T = x2.shape[0]
    emit_h = g_next is not None
    n_tiles = T // tm
    gate_blk = 3 * D_ATT // D_ATT
    conv_gate_blk = (4 * D_ATT + 2 * D_CONV) // D_CONV
    const = lambda shape: pl.BlockSpec(shape, lambda s: (0,) * len(shape), pipeline_mode=pl.Buffered(1))
    ahead = lambda blk: (lambda s: (jnp.minimum(s, n_tiles - 1), blk))
    behind = lambda s: (jnp.maximum(s - 1, 0), 0)
    return pl.pallas_call(
        functools.partial(_out_kernel, row_chunk=16, n_tiles=n_tiles, emit_h=emit_h),
        out_shape=[jax.ShapeDtypeStruct((T, D_MODEL), F32)] + [jax.ShapeDtypeStruct((T, D_MODEL), BF16)] * emit_h,
        grid=(n_tiles + 1,),
        in_specs=[
            pl.BlockSpec((tm, D_ATT), ahead(0)),
            pl.BlockSpec((tm, D_ATT), ahead(gate_blk)),
            pl.BlockSpec((tm, D_CONV), ahead(0)),
            pl.BlockSpec((tm, D_CONV), ahead(conv_gate_blk)),
            pl.BlockSpec((tm, D_MODEL), behind),
            const((1, D_ATT)), const((1, D_CONV)),
            pl.BlockSpec((None, D_CONV, D_CONV), lambda s: (layer, 0, 0), pipeline_mode=pl.Buffered(1)),
            pl.BlockSpec((None, D_MIX, D_MODEL), lambda s: (layer, 0, 0), pipeline_mode=pl.Buffered(1)),
        ] + [const((1, D_MODEL))] * emit_h,
        out_specs=[pl.BlockSpec((tm, D_MODEL), behind)] * (1 + emit_h),
        scratch_shapes=[pltpu.VMEM((tm, D_MIX), BF16), pltpu.VMEM((tm, D_MIX), BF16),
                        pltpu.VMEM((tm, D_CONV), F32)],
        compiler_params=pltpu.CompilerParams(
            dimension_semantics=("arbitrary",), vmem_limit_bytes=VMEM_LIMIT_BYTES),
        name="out_proj",
    )(att2, proj2, z2, proj2, x2, g, og, wpw_bf, w_bf, *([g_next] if emit_h else []))


def _rope_tables(seq):
    inv_freq = 1.0 / (ROPE_THETA ** (jnp.arange(0, HEAD_DIM, 2, dtype=F32) / HEAD_DIM))
    ang = jnp.arange(seq, dtype=F32)[:, None] * inv_freq[None, :]
    cos, sin = jnp.cos(ang), jnp.sin(ang)
    return jnp.concatenate([cos, cos], axis=-1), jnp.concatenate([-sin, sin], axis=-1)


def _mask_bias():
    rows = jnp.arange(Q_BLOCK)
    rows_mod4 = 4 * (rows % (Q_BLOCK // 4)) + rows // (Q_BLOCK // 4)
    ik = jnp.arange(2 * Q_BLOCK)[None, :]
    def bias(iq, shift):
        dist = iq[:, None] + shift - ik
        return jnp.where((dist >= 0) & (dist <= Q_BLOCK), 0.0, MASKED).astype(F32)
    return jnp.stack([bias(rows_mod4, 0), bias(rows_mod4, Q_BLOCK), bias(rows, 0), bias(rows, Q_BLOCK)])


def kernel(x, norm_g, w_in, q_norm_g, k_norm_g, dw_kernel, dw_bias, conv_ln_g, conv_ln_b, w_pw,
           att_out_g, conv_out_g, w_out):
    B, S, D = x.shape
    assert D == D_MODEL and S % (16 * 2 * Q_BLOCK) == 0
    depth = norm_g.shape[0]
    cos2, sin2 = _rope_tables(S)
    bias = _mask_bias()
    row = lambda v: v.reshape(1, -1)
    x2 = x.reshape(B * S, D)
    w_in_bf, w_pw_bf, w_out_bf = w_in.astype(BF16), w_pw.astype(BF16), w_out.astype(BF16)
    h2 = None
    for layer in range(depth):
        if h2 is None:
            proj2 = _norm_proj(x2, row(norm_g[layer]), w_in_bf, layer)
        else:
            proj2 = _proj(h2, w_in_bf, layer)
        proj3 = proj2.reshape(B, S, D_IN)
        att = _attention(proj3, cos2, sin2, row(q_norm_g[layer]), row(k_norm_g[layer]), bias)
        z = _conv_module(proj3, dw_kernel[layer], row(dw_bias[layer]), row(conv_ln_g[layer]),
                         row(conv_ln_b[layer]))
        g_next = row(norm_g[layer + 1]) if layer + 1 < depth else None
        outs = _out_proj(att.reshape(B * S, D_ATT), proj2, z.reshape(B * S, D_CONV), x2,
                         row(att_out_g[layer]), row(conv_out_g[layer]), w_pw_bf, w_out_bf, layer, g_next)
        x2, h2 = outs if g_next is not None else (outs[0], None)
    return x2.reshape(B, S, D)
```

```python
import functools

import jax
import jax.numpy as jnp
from jax import lax
from jax.experimental import pallas as pl
from jax.experimental.pallas import tpu as pltpu

D_MODEL = 2048
N_HEADS = 16
HEAD_DIM = 128
D_ATT = N_HEADS * HEAD_DIM
D_CONV = D_MODEL
D_MIX = D_ATT + D_CONV
D_IN = 4 * D_ATT + 3 * D_CONV
CONV_WIDTH = 31
DILATIONS = (1, 4, 16)
Q_BLOCK = 128
ROPE_THETA = 10000.0
EPS = 1e-6
MASKED = -1e30

SUBLANES = 8

VMEM_LIMIT_BYTES = 56 * 1024 * 1024

F32 = jnp.float32
BF16 = jnp.bfloat16


def _sigmoid(x):
    return 0.5 * jnp.tanh(0.5 * x) + 0.5


def _silu(x):
    h = 0.5 * x
    return h * jnp.tanh(h) + h


def _norm_proj_kernel(x_ref, g_ref, w_ref, o_ref, h_ref, *, row_chunk):
    tm = x_ref.shape[0]

    @pl.when(pl.program_id(1) == 0)
    def _():
        def body(i, carry):
            rows = pl.ds(pl.multiple_of(i * row_chunk, row_chunk), row_chunk)
            x = x_ref[rows, :]
            y = x * lax.rsqrt(jnp.mean(x * x, axis=-1, keepdims=True) + EPS)
            h_ref[rows, :] = (y * g_ref[...]).astype(BF16)
            return carry
        lax.fori_loop(0, tm // row_chunk, body, 0)

    o_ref[...] = jnp.dot(h_ref[...], w_ref[...], preferred_element_type=F32).astype(o_ref.dtype)


def _norm_proj(x2, g, w_bf, layer, *, tm=1024, tn=1024):
    T = x2.shape[0]
    return pl.pallas_call(
        functools.partial(_norm_proj_kernel, row_chunk=128),
        out_shape=jax.ShapeDtypeStruct((T, D_IN), BF16),
        grid=(T // tm, D_IN // tn),
        in_specs=[
            pl.BlockSpec((tm, D_MODEL), lambda i, j: (i, 0)),
            pl.BlockSpec((1, D_MODEL), lambda i, j: (0, 0)),
            pl.BlockSpec((None, D_MODEL, tn), lambda i, j: (layer, 0, j)),
        ],
        out_specs=pl.BlockSpec((tm, tn), lambda i, j: (i, j)),
        scratch_shapes=[pltpu.VMEM((tm, D_MODEL), BF16)],
        compiler_params=pltpu.CompilerParams(
            dimension_semantics=("arbitrary", "arbitrary"), vmem_limit_bytes=VMEM_LIMIT_BYTES),
        name="norm_proj",
    )(x2, g, w_bf)


def _proj_kernel(h_ref, w_ref, o_ref):
    o_ref[...] = jnp.dot(h_ref[...], w_ref[...], preferred_element_type=F32).astype(o_ref.dtype)


def _proj(h2, w_bf, layer, *, tm=2048, tn=1024):
    T = h2.shape[0]
    return pl.pallas_call(
        _proj_kernel,
        out_shape=jax.ShapeDtypeStruct((T, D_IN), BF16),
        grid=(T // tm, D_IN // tn),
        in_specs=[
            pl.BlockSpec((tm, D_MODEL), lambda i, j: (i, 0)),
            pl.BlockSpec((None, D_MODEL, tn), lambda i, j: (layer, 0, j)),
        ],
        out_specs=pl.BlockSpec((tm, tn), lambda i, j: (i, j)),
        compiler_params=pltpu.CompilerParams(
            dimension_semantics=("arbitrary", "arbitrary"), vmem_limit_bytes=VMEM_LIMIT_BYTES),
        name="proj",
    )(h2, w_bf)


NAT, MOD4 = 0, 1


def _attn_kernel(q_ref, k_ref, v_ref, cos_ref, sin_ref, gq_ref, gk_ref, bias_ref, perm_ref, o_ref,
                 qs, ks, vs, t0, q4, k4, v4, *stats, seq, group):
    obuf, lbuf, mbuf = stats[0:3], stats[3:6], stats[6:9]
    chunk = 256
    quarter = seq // 4

    @pl.when((pl.program_id(0) == 0) & (pl.program_id(1) == 0))
    def _():
        for layout in range(2):
            vs[layout, :, HEAD_DIM:2 * HEAD_DIM] = jnp.ones((seq, HEAD_DIM), BF16)

    def prepare(src_ref, g_ref, dst, mod4_f32, first_layout, normalize, scale):
        if normalize:
            gain = g_ref[...] * scale
            gain_rot = pltpu.roll(gain, HEAD_DIM // 2, 1)

        def body(i, carry):
            rows = pl.ds(pl.multiple_of(i * chunk, chunk), chunk)
            raw = src_ref[rows, :]
            y = raw.astype(F32)
            if normalize:
                rot = jnp.dot(raw, perm_ref[...], preferred_element_type=F32)
                inv = lax.rsqrt(jnp.mean(y * y, axis=-1, keepdims=True) + EPS)
                y = (y * gain * cos_ref[rows, :] + rot * gain_rot * sin_ref[rows, :]) * inv
            t0[rows, :] = y
            if first_layout == NAT:
                dst[NAT, rows, 0:HEAD_DIM] = y.astype(BF16)
            return carry
        lax.fori_loop(0, seq // chunk, body, 0, unroll=8)

        for r4 in range(4):
            def body4(i, carry, r4=r4):
                y = t0[pl.ds(r4 + i * (4 * chunk), chunk, stride=4), :]
                rows = pl.ds(pl.multiple_of(r4 * quarter + i * chunk, chunk), chunk)
                mod4_f32[rows, :] = y
                dst[MOD4 - first_layout, rows, 0:HEAD_DIM] = y.astype(BF16)
                return carry
            lax.fori_loop(0, quarter // chunk, body4, 0)

    prepare(q_ref, gq_ref, qs, q4, MOD4, True, HEAD_DIM ** -0.5)
    prepare(k_ref, gk_ref, ks, k4, NAT, True, 1.0)
    prepare(v_ref, None, vs, v4, NAT, False, 1.0)

    n_blocks = seq // Q_BLOCK
    piece = Q_BLOCK // 4
    for br, d in enumerate(DILATIONS):
        nb = n_blocks // d

        def one_block(c, br=br, d=d, nb=nb):
            n = lax.rem(c, nb)
            first = n == 0
            q_row = pl.multiple_of(c * Q_BLOCK, Q_BLOCK)
            k_row = pl.multiple_of(jnp.where(first, c * Q_BLOCK, (c - 1) * Q_BLOCK), Q_BLOCK)
            if d == 1:
                runs = [pl.ds(pl.multiple_of(r4 * quarter + c * piece, piece), piece) for r4 in range(4)]
                qb = jnp.concatenate([qs[0, run, :] for run in runs], axis=0)
            elif d == 4:
                qb = qs[0, pl.ds(q_row, Q_BLOCK), :]
            if d == 16:
                r = c // nb
                base = (r % 4) * quarter + r // 4
                qb = q4[pl.ds(base + n * (4 * Q_BLOCK), Q_BLOCK, stride=4), :].astype(BF16)
                keys = pl.ds(base, 2 * Q_BLOCK, stride=4)
                kb = k4[keys, :].astype(BF16)
                vb = jnp.concatenate([v4[keys, :].astype(BF16), jnp.ones((2 * Q_BLOCK, HEAD_DIM), BF16)], axis=1)
            else:
                kb = ks[br, pl.ds(k_row, 2 * Q_BLOCK), :]
                vb = vs[br, pl.ds(k_row, 2 * Q_BLOCK), :]
            s = lax.dot_general(qb, kb, (((1,), (1,)), ((), ())), preferred_element_type=F32)
            s = s + bias_ref[jnp.where(first, 0, 1) + (0 if d == 1 else 2)]
            m = jnp.max(s, axis=-1, keepdims=True)
            p = jnp.exp(s - m).astype(BF16)
            oe = jnp.dot(p, vb, preferred_element_type=F32)
            mb = jnp.broadcast_to(m, (Q_BLOCK, HEAD_DIM))
            ob, lb = oe[:, 0:HEAD_DIM], oe[:, HEAD_DIM:2 * HEAD_DIM]
            if d == 1:
                for r4, run in enumerate(runs):
                    sub = slice(r4 * piece, (r4 + 1) * piece)
                    obuf[br][run, :] = ob[sub, :]
                    lbuf[br][run, :] = lb[sub, :]
                    mbuf[br][run, :] = mb[sub, :]
            else:
                if d == 4:
                    rows = pl.ds(q_row, Q_BLOCK)
                else:
                    rows = pl.ds(base + n * (4 * Q_BLOCK), Q_BLOCK, stride=4)
                obuf[br][rows, :] = ob
                lbuf[br][rows, :] = lb
                mbuf[br][rows, :] = mb

        def block_group(it, carry, one_block=one_block):
            for g in range(group):
                one_block(it * group + g)
            return carry
        lax.fori_loop(0, n_blocks // group, block_group, 0)

    for r4 in range(4):
        def finish(i, carry, r4=r4):
            rows = pl.ds(pl.multiple_of(r4 * quarter + i * chunk, chunk), chunk)
            ms = [mbuf[br][rows, :] for br in range(3)]
            m_all = jnp.maximum(jnp.maximum(ms[0], ms[1]), ms[2])
            num = den = None
            for br in range(3):
                w = jnp.exp(ms[br] - m_all)
                t_num = w * obuf[br][rows, :]
                t_den = w * lbuf[br][rows, :]
                num = t_num if num is None else num + t_num
                den = t_den if den is None else den + t_den
            o_ref[pl.ds(r4 + i * (4 * chunk), chunk, stride=4), :] = (num / den).astype(o_ref.dtype)
            return carry
        lax.fori_loop(0, quarter // chunk, finish, 0, unroll=2)


def _attention(proj3, cos2, sin2, gq, gk, bias, *, group=32):
    B, S, _ = proj3.shape
    head_spec = lambda off: pl.BlockSpec((None, S, HEAD_DIM), lambda b, h: (b, 0, off + h))
    const = lambda shape: pl.BlockSpec(shape, lambda b, h: (0,) * len(shape), pipeline_mode=pl.Buffered(1))
    return pl.pallas_call(
        functools.partial(_attn_kernel, seq=S, group=group),
        out_shape=jax.ShapeDtypeStruct((B, S, D_ATT), F32),
        grid=(B, N_HEADS),
        in_specs=[
            head_spec(0), head_spec(N_HEADS), head_spec(2 * N_HEADS),
            const((S, HEAD_DIM)), const((S, HEAD_DIM)),
            const((1, HEAD_DIM)), const((1, HEAD_DIM)),
            const((4, Q_BLOCK, 2 * Q_BLOCK)),
            const((HEAD_DIM, HEAD_DIM)),
        ],
        out_specs=pl.BlockSpec((None, S, HEAD_DIM), lambda b, h: (b, 0, h)),
        scratch_shapes=[
            pltpu.VMEM((1, S, HEAD_DIM), BF16), pltpu.VMEM((2, S, HEAD_DIM), BF16),
            pltpu.VMEM((2, S, 2 * HEAD_DIM), BF16),
        ] + [pltpu.VMEM((S, HEAD_DIM), F32)] * (4 + 9),
        compiler_params=pltpu.CompilerParams(
            dimension_semantics=("arbitrary", "arbitrary"), vmem_limit_bytes=VMEM_LIMIT_BYTES),
        name="dilated_attention",
    )(proj3, proj3, proj3, cos2, sin2, gq, gk, bias,
      jnp.roll(jnp.eye(HEAD_DIM, dtype=BF16), HEAD_DIM // 2, axis=1))


CONV_HALO = 32


def _conv_kernel(a_ref, b_ref, dw_ref, dwb_ref, lng_ref, lnb_ref, o_ref, u_ext, y_s, *, row_chunk, conv_rows):
    tm = a_ref.shape[0]

    def row_loop(body, n_rows):
        def step(i, carry):
            body(pl.ds(pl.multiple_of(i * n_rows, n_rows), n_rows),
                 pl.ds(pl.multiple_of(i * n_rows + CONV_HALO, SUBLANES), n_rows))
            return carry
        lax.fori_loop(0, tm // n_rows, step, 0)

    @pl.when(pl.program_id(1) == 0)
    def _():
        u_ext[0:CONV_HALO, :] = jnp.zeros((CONV_HALO, D_CONV), F32)

    def glu(rows, ext_rows):
        u_ext[ext_rows, :] = a_ref[rows, :].astype(F32) * _sigmoid(b_ref[rows, :].astype(F32))
    row_loop(glu, 2 * row_chunk)

    lead = CONV_HALO - (CONV_WIDTH - 1)

    def lanes_body(cc, carry):
        lanes = pl.ds(pl.multiple_of(cc * 128, 128), 128)
        for rc in range(tm // conv_rows):
            r0 = rc * conv_rows
            acc = jnp.broadcast_to(dwb_ref[:, lanes], (conv_rows, 128))
            for shift in range(SUBLANES):
                n_rows = conv_rows + (SUBLANES if shift else 0)
                part = None
                for off in range(shift, lead + CONV_WIDTH, SUBLANES):
                    if off < lead:
                        continue
                    start = r0 + off - shift
                    term = dw_ref[off - lead:off - lead + 1, lanes] * u_ext[start:start + n_rows, lanes]
                    part = term if part is None else part + term
                acc = acc + part[shift:shift + conv_rows, :]
            y_s[r0:r0 + conv_rows, lanes] = acc
        return carry
    lax.fori_loop(0, D_CONV // 128, lanes_body, 0)

    u_ext[0:CONV_HALO, :] = u_ext[tm:tm + CONV_HALO, :]

    def layer_norm_swish(rows, ext_rows):
        y = y_s[rows, :]
        yc = y - jnp.mean(y, axis=-1, keepdims=True)
        yn = yc * lax.rsqrt(jnp.mean(yc * yc, axis=-1, keepdims=True) + EPS)
        o_ref[rows, :] = _silu(yn * lng_ref[...] + lnb_ref[...]).astype(o_ref.dtype)
    row_loop(layer_norm_swish, 2 * row_chunk)


def _conv_module(proj3, dw, dwb, lng, lnb, *, tm=512):
    B, S, _ = proj3.shape
    col = lambda blk: pl.BlockSpec((None, tm, D_CONV), lambda b, i: (b, i, blk))
    const = lambda shape: pl.BlockSpec(shape, lambda b, i: (0,) * len(shape), pipeline_mode=pl.Buffered(1))
    base = 4 * D_ATT // D_CONV
    return pl.pallas_call(
        functools.partial(_conv_kernel, row_chunk=64, conv_rows=128),
        out_shape=jax.ShapeDtypeStruct((B, S, D_CONV), BF16),
        grid=(B, S // tm),
        in_specs=[
            col(base), col(base + 1),
            const((CONV_WIDTH, D_CONV)), const((1, D_CONV)), const((1, D_CONV)), const((1, D_CONV)),
        ],
        out_specs=pl.BlockSpec((None, tm, D_CONV), lambda b, i: (b, i, 0)),
        scratch_shapes=[pltpu.VMEM((tm + CONV_HALO, D_CONV), F32), pltpu.VMEM((tm, D_CONV), F32)],
        compiler_params=pltpu.CompilerParams(
            dimension_semantics=("arbitrary", "arbitrary"), vmem_limit_bytes=VMEM_LIMIT_BYTES),
        name="conformer_conv",
    )(proj3, proj3, dw, dwb, lng, lnb)


def _out_kernel(att_ref, ga_ref, z_ref, gc_ref, x_ref, g_ref, og_ref, wpw_ref, w_ref, *rest, row_chunk, emit_h):
    if emit_h:
        gn_ref, o_ref, h_ref, lhs, y_pw = rest
    else:
        o_ref, lhs, y_pw = rest
    tm = att_ref.shape[0]
    chunks = [pl.ds(i * row_chunk, row_chunk) for i in range(tm // row_chunk)]

    y_pw[...] = jnp.dot(z_ref[...], wpw_ref[...], preferred_element_type=F32)

    for rows in chunks:
        a = att_ref[rows, :]
        an = a * lax.rsqrt(jnp.mean(a * a, axis=-1, keepdims=True) + EPS) * g_ref[...]
        lhs[rows, 0:D_ATT] = (an * _silu(ga_ref[rows, :].astype(F32))).astype(BF16)

    for rows in chunks:
        c = y_pw[rows, :]
        cn = c * lax.rsqrt(jnp.mean(c * c, axis=-1, keepdims=True) + EPS) * og_ref[...]
        lhs[rows, D_ATT:D_MIX] = (cn * _silu(gc_ref[rows, :].astype(F32))).astype(BF16)

    o_ref[...] = x_ref[...] + jnp.dot(lhs[...], w_ref[...], preferred_element_type=F32)

    if emit_h:
        for rows in chunks:
            xn = o_ref[rows, :]
            hn = xn * lax.rsqrt(jnp.mean(xn * xn, axis=-1, keepdims=True) + EPS)
            h_ref[rows, :] = (hn * gn_ref[...]).astype(BF16)


def _out_proj(att2, proj2, z2, x2, g, og, wpw_bf, w_bf, layer, g_next=None, *, tm=256):
    T = x2.shape[0]
    emit_h = g_next is not None
    gate_blk = 3 * D_ATT // D_ATT
    conv_gate_blk = (4 * D_ATT + 2 * D_CONV) // D_CONV
    const = lambda shape: pl.BlockSpec(shape, lambda i: (0,) * len(shape), pipeline_mode=pl.Buffered(1))
    tile = lambda blk: pl.BlockSpec((tm, D_MODEL), lambda i: (i, blk))
    return pl.pallas_call(
        functools.partial(_out_kernel, row_chunk=16, emit_h=emit_h),
        out_shape=[jax.ShapeDtypeStruct((T, D_MODEL), F32)] + [jax.ShapeDtypeStruct((T, D_MODEL), BF16)] * emit_h,
        grid=(T // tm,),
        in_specs=[
            tile(0), tile(gate_blk), tile(0), tile(conv_gate_blk), tile(0),
            const((1, D_ATT)), const((1, D_CONV)),
            pl.BlockSpec((None, D_CONV, D_CONV), lambda i: (layer, 0, 0), pipeline_mode=pl.Buffered(1)),
            pl.BlockSpec((None, D_MIX, D_MODEL), lambda i: (layer, 0, 0), pipeline_mode=pl.Buffered(1)),
        ] + [const((1, D_MODEL))] * emit_h,
        out_specs=[tile(0)] * (1 + emit_h),
        scratch_shapes=[pltpu.VMEM((tm, D_MIX), BF16), pltpu.VMEM((tm, D_CONV), F32)],
        compiler_params=pltpu.CompilerParams(
            dimension_semantics=("arbitrary",), vmem_limit_bytes=VMEM_LIMIT_BYTES),
        name="out_proj",
    )(att2, proj2, z2, proj2, x2, g, og, wpw_bf, w_bf, *([g_next] if emit_h else []))


def _rope_tables(seq):
    inv_freq = 1.0 / (ROPE_THETA ** (jnp.arange(0, HEAD_DIM, 2, dtype=F32) / HEAD_DIM))
    ang = jnp.arange(seq, dtype=F32)[:, None] * inv_freq[None, :]
    cos, sin = jnp.cos(ang), jnp.sin(ang)
    return jnp.concatenate([cos, cos], axis=-1), jnp.concatenate([-sin, sin], axis=-1)


def _mask_bias():
    rows = jnp.arange(Q_BLOCK)
    rows_mod4 = 4 * (rows % (Q_BLOCK // 4)) + rows // (Q_BLOCK // 4)
    ik = jnp.arange(2 * Q_BLOCK)[None, :]
    def bias(iq, shift):
        dist = iq[:, None] + shift - ik
        return jnp.where((dist >= 0) & (dist <= Q_BLOCK), 0.0, MASKED).astype(F32)
    return jnp.stack([bias(rows_mod4, 0), bias(rows_mod4, Q_BLOCK), bias(rows, 0), bias(rows, Q_BLOCK)])


def kernel(x, norm_g, w_in, q_norm_g, k_norm_g, dw_kernel, dw_bias, conv_ln_g, conv_ln_b, w_pw,
           att_out_g, conv_out_g, w_out):
    B, S, D = x.shape
    assert D == D_MODEL and S % (16 * 2 * Q_BLOCK) == 0
    depth = norm_g.shape[0]
    cos2, sin2 = _rope_tables(S)
    bias = _mask_bias()
    row = lambda v: v.reshape(1, -1)
    x2 = x.reshape(B * S, D)
    w_in_bf, w_pw_bf, w_out_bf = w_in.astype(BF16), w_pw.astype(BF16), w_out.astype(BF16)
    h2 = None
    for layer in range(depth):
        if h2 is None:
            proj2 = _norm_proj(x2, row(norm_g[layer]), w_in_bf, layer)
        else:
            proj2 = _proj(h2, w_in_bf, layer)
        proj3 = proj2.reshape(B, S, D_IN)
        att = _attention(proj3, cos2, sin2, row(q_norm_g[layer]), row(k_norm_g[layer]), bias)
        z = _conv_module(proj3, dw_kernel[layer], row(dw_bias[layer]), row(conv_ln_g[layer]),
                         row(conv_ln_b[layer]))
        g_next = row(norm_g[layer + 1]) if layer + 1 < depth else None
        outs = _out_proj(att.reshape(B * S, D_ATT), proj2, z.reshape(B * S, D_CONV), x2,
                         row(att_out_g[layer]), row(conv_out_g[layer]), w_pw_bf, w_out_bf, layer, g_next)
        x2, h2 = outs if g_next is not None else (outs[0], None)
    return x2.reshape(B, S, D)
```

```python
import functools

import jax
import jax.numpy as jnp
from jax import lax
from jax.experimental import pallas as pl
from jax.experimental.pallas import tpu as pltpu

D_MODEL = 2048
N_HEADS = 16
HEAD_DIM = 128
D_ATT = N_HEADS * HEAD_DIM
D_CONV = D_MODEL
D_MIX = D_ATT + D_CONV
D_IN = 4 * D_ATT + 3 * D_CONV
CONV_WIDTH = 31
DILATIONS = (1, 4, 16)
Q_BLOCK = 128
ROPE_THETA = 10000.0
EPS = 1e-6
MASKED = -1e30

SUBLANES = 8

VMEM_LIMIT_BYTES = 56 * 1024 * 1024

F32 = jnp.float32
BF16 = jnp.bfloat16


def _sigmoid(x):
    return 0.5 * jnp.tanh(0.5 * x) + 0.5


def _silu(x):
    h = 0.5 * x
    return h * jnp.tanh(h) + h


def _norm_proj_kernel(x_ref, g_ref, w_ref, o_ref, h_ref, *, row_chunk):
    tm = x_ref.shape[0]

    @pl.when(pl.program_id(1) == 0)
    def _():
        def body(i, carry):
            rows = pl.ds(pl.multiple_of(i * row_chunk, row_chunk), row_chunk)
            x = x_ref[rows, :]
            y = x * lax.rsqrt(jnp.mean(x * x, axis=-1, keepdims=True) + EPS)
            h_ref[rows, :] = (y * g_ref[...]).astype(BF16)
            return carry
        lax.fori_loop(0, tm // row_chunk, body, 0)

    o_ref[...] = jnp.dot(h_ref[...], w_ref[...], preferred_element_type=F32).astype(o_ref.dtype)


def _norm_proj(x2, g, w_bf, layer, *, tm=1024, tn=1024):
    T = x2.shape[0]
    return pl.pallas_call(
        functools.partial(_norm_proj_kernel, row_chunk=128),
        out_shape=jax.ShapeDtypeStruct((T, D_IN), BF16),
        grid=(T // tm, D_IN // tn),
        in_specs=[
            pl.BlockSpec((tm, D_MODEL), lambda i, j: (i, 0)),
            pl.BlockSpec((1, D_MODEL), lambda i, j: (0, 0)),
            pl.BlockSpec((None, D_MODEL, tn), lambda i, j: (layer, 0, j)),
        ],
        out_specs=pl.BlockSpec((tm, tn), lambda i, j: (i, j)),
        scratch_shapes=[pltpu.VMEM((tm, D_MODEL), BF16)],
        compiler_params=pltpu.CompilerParams(
            dimension_semantics=("arbitrary", "arbitrary"), vmem_limit_bytes=VMEM_LIMIT_BYTES),
        name="norm_proj",
    )(x2, g, w_bf)


def _proj_kernel(h_ref, w_ref, o_ref):
    o_ref[...] = jnp.dot(h_ref[...], w_ref[...], preferred_element_type=F32).astype(o_ref.dtype)


def _proj(h2, w_bf, layer, *, tm=2048, tn=1024):
    T = h2.shape[0]
    return pl.pallas_call(
        _proj_kernel,
        out_shape=jax.ShapeDtypeStruct((T, D_IN), BF16),
        grid=(T // tm, D_IN // tn),
        in_specs=[
            pl.BlockSpec((tm, D_MODEL), lambda i, j: (i, 0)),
            pl.BlockSpec((None, D_MODEL, tn), lambda i, j: (layer, 0, j)),
        ],
        out_specs=pl.BlockSpec((tm, tn), lambda i, j: (i, j)),
        compiler_params=pltpu.CompilerParams(
            dimension_semantics=("arbitrary", "arbitrary"), vmem_limit_bytes=VMEM_LIMIT_BYTES),
        name="proj",
    )(h2, w_bf)


NAT, MOD4 = 0, 1


def _attn_kernel(q_ref, k_ref, v_ref, cos_ref, sin_ref, gq_ref, gk_ref, bias_ref, perm_ref, o_ref,
                 qs, ks, vs, t0, q4, k4, v4, *stats, seq, group):
    obuf, lbuf, mbuf = stats[0:3], stats[3:6], stats[6:9]
    chunk = 256
    quarter = seq // 4

    @pl.when((pl.program_id(0) == 0) & (pl.program_id(1) == 0))
    def _():
        for layout in range(2):
            vs[layout, :, HEAD_DIM:2 * HEAD_DIM] = jnp.ones((seq, HEAD_DIM), BF16)

    def prepare(src_ref, g_ref, dst, mod4_f32, first_layout, normalize, scale):
        if normalize:
            gain = g_ref[...] * scale
            gain_rot = pltpu.roll(gain, HEAD_DIM // 2, 1)

        def body(i, carry):
            rows = pl.ds(pl.multiple_of(i * chunk, chunk), chunk)
            raw = src_ref[rows, :]
            y = raw.astype(F32)
            if normalize:
                rot = jnp.dot(raw, perm_ref[...], preferred_element_type=F32)
                inv = lax.rsqrt(jnp.mean(y * y, axis=-1, keepdims=True) + EPS)
                y = (y * gain * cos_ref[rows, :] + rot * gain_rot * sin_ref[rows, :]) * inv
            t0[rows, :] = y
            if first_layout == NAT:
                dst[NAT, rows, 0:HEAD_DIM] = y.astype(BF16)
            return carry
        lax.fori_loop(0, seq // chunk, body, 0, unroll=8)

        for r4 in range(4):
            def body4(i, carry, r4=r4):
                y = t0[pl.ds(r4 + i * (4 * chunk), chunk, stride=4), :]
                rows = pl.ds(pl.multiple_of(r4 * quarter + i * chunk, chunk), chunk)
                mod4_f32[rows, :] = y
                dst[MOD4 - first_layout, rows, 0:HEAD_DIM] = y.astype(BF16)
                return carry
            lax.fori_loop(0, quarter // chunk, body4, 0, unroll=True)

    prepare(q_ref, gq_ref, qs, q4, MOD4, True, HEAD_DIM ** -0.5)
    prepare(k_ref, gk_ref, ks, k4, NAT, True, 1.0)
    prepare(v_ref, None, vs, v4, NAT, False, 1.0)

    n_blocks = seq // Q_BLOCK
    piece = Q_BLOCK // 4
    for br, d in enumerate(DILATIONS):
        nb = n_blocks // d

        def one_block(c, br=br, d=d, nb=nb):
            n = lax.rem(c, nb)
            first = n == 0
            q_row = pl.multiple_of(c * Q_BLOCK, Q_BLOCK)
            k_row = pl.multiple_of(jnp.where(first, c * Q_BLOCK, (c - 1) * Q_BLOCK), Q_BLOCK)
            if d == 1:
                runs = [pl.ds(pl.multiple_of(r4 * quarter + c * piece, piece), piece) for r4 in range(4)]
                qb = jnp.concatenate([qs[0, run, :] for run in runs], axis=0)
            elif d == 4:
                qb = qs[0, pl.ds(q_row, Q_BLOCK), :]
            if d == 16:
                r = c // nb
                base = (r % 4) * quarter + r // 4
                qb = q4[pl.ds(base + n * (4 * Q_BLOCK), Q_BLOCK, stride=4), :].astype(BF16)
                keys = pl.ds(base, 2 * Q_BLOCK, stride=4)
                kb = k4[keys, :].astype(BF16)
                vb = jnp.concatenate([v4[keys, :].astype(BF16), jnp.ones((2 * Q_BLOCK, HEAD_DIM), BF16)], axis=1)
            else:
                kb = ks[br, pl.ds(k_row, 2 * Q_BLOCK), :]
                vb = vs[br, pl.ds(k_row, 2 * Q_BLOCK), :]
            s = lax.dot_general(qb, kb, (((1,), (1,)), ((), ())), preferred_element_type=F32)
            s = s + bias_ref[jnp.where(first, 0, 1) + (0 if d == 1 else 2)]
            m = jnp.max(s, axis=-1, keepdims=True)
            p = jnp.exp(s - m).astype(BF16)
            oe = jnp.dot(p, vb, preferred_element_type=F32)
            mb = jnp.broadcast_to(m, (Q_BLOCK, HEAD_DIM))
            ob, lb = oe[:, 0:HEAD_DIM], oe[:, HEAD_DIM:2 * HEAD_DIM]
            if d == 1:
                for r4, run in enumerate(runs):
                    sub = slice(r4 * piece, (r4 + 1) * piece)
                    obuf[br][run, :] = ob[sub, :]
                    lbuf[br][run, :] = lb[sub, :]
                    mbuf[br][run, :] = mb[sub, :]
            else:
                if d == 4:
                    rows = pl.ds(q_row, Q_BLOCK)
                else:
                    rows = pl.ds(base + n * (4 * Q_BLOCK), Q_BLOCK, stride=4)
                obuf[br][rows, :] = ob
                lbuf[br][rows, :] = lb
                mbuf[br][rows, :] = mb

        def block_group(it, carry, one_block=one_block):
            for g in range(group):
                one_block(it * group + g)
            return carry
        lax.fori_loop(0, n_blocks // group, block_group, 0)

    for r4 in range(4):
        def finish(i, carry, r4=r4):
            rows = pl.ds(pl.multiple_of(r4 * quarter + i * chunk, chunk), chunk)
            ms = [mbuf[br][rows, :] for br in range(3)]
            m_all = jnp.maximum(jnp.maximum(ms[0], ms[1]), ms[2])
            num = den = None
            for br in range(3):
                w = jnp.exp(ms[br] - m_all)
                t_num = w * obuf[br][rows, :]
                t_den = w * lbuf[br][rows, :]
                num = t_num if num is None else num + t_num
                den = t_den if den is None else den + t_den
            o_ref[pl.ds(r4 + i * (4 * chunk), chunk, stride=4), :] = (num / den).astype(o_ref.dtype)
            return carry
        lax.fori_loop(0, quarter // chunk, finish, 0, unroll=2)


def _attention(proj3, cos2, sin2, gq, gk, bias, *, group=32):
    B, S, _ = proj3.shape
    head_spec = lambda off: pl.BlockSpec((None, S, HEAD_DIM), lambda b, h: (b, 0, off + h))
    const = lambda shape: pl.BlockSpec(shape, lambda b, h: (0,) * len(shape), pipeline_mode=pl.Buffered(1))
    return pl.pallas_call(
        functools.partial(_attn_kernel, seq=S, group=group),
        out_shape=jax.ShapeDtypeStruct((B, S, D_ATT), F32),
        grid=(B, N_HEADS),
        in_specs=[
            head_spec(0), head_spec(N_HEADS), head_spec(2 * N_HEADS),
            const((S, HEAD_DIM)), const((S, HEAD_DIM)),
            const((1, HEAD_DIM)), const((1, HEAD_DIM)),
            const((4, Q_BLOCK, 2 * Q_BLOCK)),
            const((HEAD_DIM, HEAD_DIM)),
        ],
        out_specs=pl.BlockSpec((None, S, HEAD_DIM), lambda b, h: (b, 0, h)),
        scratch_shapes=[
            pltpu.VMEM((1, S, HEAD_DIM), BF16), pltpu.VMEM((2, S, HEAD_DIM), BF16),
            pltpu.VMEM((2, S, 2 * HEAD_DIM), BF16),
        ] + [pltpu.VMEM((S, HEAD_DIM), F32)] * (4 + 9),
        compiler_params=pltpu.CompilerParams(
            dimension_semantics=("arbitrary", "arbitrary"), vmem_limit_bytes=VMEM_LIMIT_BYTES),
        name="dilated_attention",
    )(proj3, proj3, proj3, cos2, sin2, gq, gk, bias,
      jnp.roll(jnp.eye(HEAD_DIM, dtype=BF16), HEAD_DIM // 2, axis=1))


CONV_HALO = 32


def _conv_kernel(a_ref, b_ref, dw_ref, dwb_ref, lng_ref, lnb_ref, o_ref, u_ext, y_s, *, row_chunk, conv_rows):
    tm = a_ref.shape[0]

    def row_loop(body, n_rows):
        def step(i, carry):
            body(pl.ds(pl.multiple_of(i * n_rows, n_rows), n_rows),
                 pl.ds(pl.multiple_of(i * n_rows + CONV_HALO, SUBLANES), n_rows))
            return carry
        lax.fori_loop(0, tm // n_rows, step, 0)

    @pl.when(pl.program_id(1) == 0)
    def _():
        u_ext[0:CONV_HALO, :] = jnp.zeros((CONV_HALO, D_CONV), F32)

    def glu(rows, ext_rows):
        u_ext[ext_rows, :] = a_ref[rows, :].astype(F32) * _sigmoid(b_ref[rows, :].astype(F32))
    row_loop(glu, 2 * row_chunk)

    lead = CONV_HALO - (CONV_WIDTH - 1)

    def lanes_body(cc, carry):
        lanes = pl.ds(pl.multiple_of(cc * 128, 128), 128)
        for rc in range(tm // conv_rows):
            r0 = rc * conv_rows
            acc = jnp.broadcast_to(dwb_ref[:, lanes], (conv_rows, 128))
            for shift in range(SUBLANES):
                n_rows = conv_rows + (SUBLANES if shift else 0)
                part = None
                for off in range(shift, lead + CONV_WIDTH, SUBLANES):
                    if off < lead:
                        continue
                    start = r0 + off - shift
                    term = dw_ref[off - lead:off - lead + 1, lanes] * u_ext[start:start + n_rows, lanes]
                    part = term if part is None else part + term
                acc = acc + part[shift:shift + conv_rows, :]
            y_s[r0:r0 + conv_rows, lanes] = acc
        return carry
    lax.fori_loop(0, D_CONV // 128, lanes_body, 0)

    u_ext[0:CONV_HALO, :] = u_ext[tm:tm + CONV_HALO, :]

    def layer_norm_swish(rows, ext_rows):
        y = y_s[rows, :]
        yc = y - jnp.mean(y, axis=-1, keepdims=True)
        yn = yc * lax.rsqrt(jnp.mean(yc * yc, axis=-1, keepdims=True) + EPS)
        o_ref[rows, :] = _silu(yn * lng_ref[...] + lnb_ref[...]).astype(o_ref.dtype)
    row_loop(layer_norm_swish, 2 * row_chunk)


def _conv_module(proj3, dw, dwb, lng, lnb, *, tm=512):
    B, S, _ = proj3.shape
    col = lambda blk: pl.BlockSpec((None, tm, D_CONV), lambda b, i: (b, i, blk))
    const = lambda shape: pl.BlockSpec(shape, lambda b, i: (0,) * len(shape), pipeline_mode=pl.Buffered(1))
    base = 4 * D_ATT // D_CONV
    return pl.pallas_call(
        functools.partial(_conv_kernel, row_chunk=64, conv_rows=128),
        out_shape=jax.ShapeDtypeStruct((B, S, D_CONV), BF16),
        grid=(B, S // tm),
        in_specs=[
            col(base), col(base + 1),
            const((CONV_WIDTH, D_CONV)), const((1, D_CONV)), const((1, D_CONV)), const((1, D_CONV)),
        ],
        out_specs=pl.BlockSpec((None, tm, D_CONV), lambda b, i: (b, i, 0)),
        scratch_shapes=[pltpu.VMEM((tm + CONV_HALO, D_CONV), F32), pltpu.VMEM((tm, D_CONV), F32)],
        compiler_params=pltpu.CompilerParams(
            dimension_semantics=("arbitrary", "arbitrary"), vmem_limit_bytes=VMEM_LIMIT_BYTES),
        name="conformer_conv",
    )(proj3, proj3, dw, dwb, lng, lnb)


def _out_kernel(att_ref, ga_ref, z_ref, gc_ref, x_ref, g_ref, og_ref, wpw_ref, w_ref, *rest, row_chunk, emit_h):
    if emit_h:
        gn_ref, o_ref, h_ref, lhs, y_pw = rest
    else:
        o_ref, lhs, y_pw = rest
    tm = att_ref.shape[0]
    chunks = [pl.ds(i * row_chunk, row_chunk) for i in range(tm // row_chunk)]

    y_pw[...] = jnp.dot(z_ref[...], wpw_ref[...], preferred_element_type=F32)

    for rows in chunks:
        a = att_ref[rows, :]
        an = a * lax.rsqrt(jnp.mean(a * a, axis=-1, keepdims=True) + EPS) * g_ref[...]
        lhs[rows, 0:D_ATT] = (an * _silu(ga_ref[rows, :].astype(F32))).astype(BF16)

    for rows in chunks:
        c = y_pw[rows, :]
        cn = c * lax.rsqrt(jnp.mean(c * c, axis=-1, keepdims=True) + EPS) * og_ref[...]
        lhs[rows, D_ATT:D_MIX] = (cn * _silu(gc_ref[rows, :].astype(F32))).astype(BF16)

    o_ref[...] = x_ref[...] + jnp.dot(lhs[...], w_ref[...], preferred_element_type=F32)

    if emit_h:
        for rows in chunks:
            xn = o_ref[rows, :]
            hn = xn * lax.rsqrt(jnp.mean(xn * xn, axis=-1, keepdims=True) + EPS)
            h_ref[rows, :] = (hn * gn_ref[...]).astype(BF16)


def _out_proj(att2, proj2, z2, x2, g, og, wpw_bf, w_bf, layer, g_next=None, *, tm=256):
    T = x2.shape[0]
    emit_h = g_next is not None
    gate_blk = 3 * D_ATT // D_ATT
    conv_gate_blk = (4 * D_ATT + 2 * D_CONV) // D_CONV
    const = lambda shape: pl.BlockSpec(shape, lambda i: (0,) * len(shape), pipeline_mode=pl.Buffered(1))
    tile = lambda blk: pl.BlockSpec((tm, D_MODEL), lambda i: (i, blk))
    return pl.pallas_call(
        functools.partial(_out_kernel, row_chunk=16, emit_h=emit_h),
        out_shape=[jax.ShapeDtypeStruct((T, D_MODEL), F32)] + [jax.ShapeDtypeStruct((T, D_MODEL), BF16)] * emit_h,
        grid=(T // tm,),
        in_specs=[
            tile(0), tile(gate_blk), tile(0), tile(conv_gate_blk), tile(0),
            const((1, D_ATT)), const((1, D_CONV)),
            pl.BlockSpec((None, D_CONV, D_CONV), lambda i: (layer, 0, 0), pipeline_mode=pl.Buffered(1)),
            pl.BlockSpec((None, D_MIX, D_MODEL), lambda i: (layer, 0, 0), pipeline_mode=pl.Buffered(1)),
        ] + [const((1, D_MODEL))] * emit_h,
        out_specs=[tile(0)] * (1 + emit_h),
        scratch_shapes=[pltpu.VMEM((tm, D_MIX), BF16), pltpu.VMEM((tm, D_CONV), F32)],
        compiler_params=pltpu.CompilerParams(
            dimension_semantics=("arbitrary",), vmem_limit_bytes=VMEM_LIMIT_BYTES),
        name="out_proj",
    )(att2, proj2, z2, proj2, x2, g, og, wpw_bf, w_bf, *([g_next] if emit_h else []))


def _rope_tables(seq):
    inv_freq = 1.0 / (ROPE_THETA ** (jnp.arange(0, HEAD_DIM, 2, dtype=F32) / HEAD_DIM))
    ang = jnp.arange(seq, dtype=F32)[:, None] * inv_freq[None, :]
    cos, sin = jnp.cos(ang), jnp.sin(ang)
    return jnp.concatenate([cos, cos], axis=-1), jnp.concatenate([-sin, sin], axis=-1)


def _mask_bias():
    rows = jnp.arange(Q_BLOCK)
    rows_mod4 = 4 * (rows % (Q_BLOCK // 4)) + rows // (Q_BLOCK // 4)
    ik = jnp.arange(2 * Q_BLOCK)[None, :]
    def bias(iq, shift):
        dist = iq[:, None] + shift - ik
        return jnp.where((dist >= 0) & (dist <= Q_BLOCK), 0.0, MASKED).astype(F32)
    return jnp.stack([bias(rows_mod4, 0), bias(rows_mod4, Q_BLOCK), bias(rows, 0), bias(rows, Q_BLOCK)])


def kernel(x, norm_g, w_in, q_norm_g, k_norm_g, dw_kernel, dw_bias, conv_ln_g, conv_ln_b, w_pw,
           att_out_g, conv_out_g, w_out):
    B, S, D = x.shape
    assert D == D_MODEL and S % (16 * 2 * Q_BLOCK) == 0
    depth = norm_g.shape[0]
    cos2, sin2 = _rope_tables(S)
    bias = _mask_bias()
    row = lambda v: v.reshape(1, -1)
    x2 = x.reshape(B * S, D)
    w_in_bf, w_pw_bf, w_out_bf = w_in.astype(BF16), w_pw.astype(BF16), w_out.astype(BF16)
    h2 = None
    for layer in range(depth):
        if h2 is None:
            proj2 = _norm_proj(x2, row(norm_g[layer]), w_in_bf, layer)
        else:
            proj2 = _proj(h2, w_in_bf, layer)
        proj3 = proj2.reshape(B, S, D_IN)
        att = _attention(proj3, cos2, sin2, row(q_norm_g[layer]), row(k_norm_g[layer]), bias)
        z = _conv_module(proj3, dw_kernel[layer], row(dw_bias[layer]), row(conv_ln_g[layer]),
                         row(conv_ln_b[layer]))
        g_next = row(norm_g[layer + 1]) if layer + 1 < depth else None
        outs = _out_proj(att.reshape(B * S, D_ATT), proj2, z.reshape(B * S, D_CONV), x2,
                         row(att_out_g[layer]), row(conv_out_g[layer]), w_pw_bf, w_out_bf, layer, g_next)
        x2, h2 = outs if g_next is not None else (outs[0], None)
    return x2.reshape(B, S, D)
```

```python
import functools

import jax
import jax.numpy as jnp
from jax import lax
from jax.experimental import pallas as pl
from jax.experimental.pallas import tpu as pltpu

D_MODEL = 2048
N_HEADS = 16
HEAD_DIM = 128
D_ATT = N_HEADS * HEAD_DIM
D_CONV = D_MODEL
D_MIX = D_ATT + D_CONV
D_IN = 4 * D_ATT + 3 * D_CONV
CONV_WIDTH = 31
DILATIONS = (1, 4, 16)
Q_BLOCK = 128
ROPE_THETA = 10000.0
EPS = 1e-6
MASKED = -1e30

SUBLANES = 8

VMEM_LIMIT_BYTES = 56 * 1024 * 1024

F32 = jnp.float32
BF16 = jnp.bfloat16


def _sigmoid(x):
    return 0.5 * jnp.tanh(0.5 * x) + 0.5


def _silu(x):
    h = 0.5 * x
    return h * jnp.tanh(h) + h


def _norm_proj_kernel(x_ref, g_ref, w_ref, o_ref, h_ref, *, row_chunk):
    tm = x_ref.shape[0]

    @pl.when(pl.program_id(1) == 0)
    def _():
        def body(i, carry):
            rows = pl.ds(pl.multiple_of(i * row_chunk, row_chunk), row_chunk)
            x = x_ref[rows, :]
            y = x * lax.rsqrt(jnp.mean(x * x, axis=-1, keepdims=True) + EPS)
            h_ref[rows, :] = (y * g_ref[...]).astype(BF16)
            return carry
        lax.fori_loop(0, tm // row_chunk, body, 0)

    o_ref[...] = jnp.dot(h_ref[...], w_ref[...], preferred_element_type=F32).astype(o_ref.dtype)


def _norm_proj(x2, g, w_bf, layer, *, tm=1024, tn=1024):
    T = x2.shape[0]
    return pl.pallas_call(
        functools.partial(_norm_proj_kernel, row_chunk=128),
        out_shape=jax.ShapeDtypeStruct((T, D_IN), BF16),
        grid=(T // tm, D_IN // tn),
        in_specs=[
            pl.BlockSpec((tm, D_MODEL), lambda i, j: (i, 0)),
            pl.BlockSpec((1, D_MODEL), lambda i, j: (0, 0)),
            pl.BlockSpec((None, D_MODEL, tn), lambda i, j: (layer, 0, j)),
        ],
        out_specs=pl.BlockSpec((tm, tn), lambda i, j: (i, j)),
        scratch_shapes=[pltpu.VMEM((tm, D_MODEL), BF16)],
        compiler_params=pltpu.CompilerParams(
            dimension_semantics=("arbitrary", "arbitrary"), vmem_limit_bytes=VMEM_LIMIT_BYTES),
        name="norm_proj",
    )(x2, g, w_bf)


def _proj_kernel(h_ref, w_ref, o_ref):
    o_ref[...] = jnp.dot(h_ref[...], w_ref[...], preferred_element_type=F32).astype(o_ref.dtype)


def _proj(h2, w_bf, layer, *, tm=2048, tn=1024):
    T = h2.shape[0]
    return pl.pallas_call(
        _proj_kernel,
        out_shape=jax.ShapeDtypeStruct((T, D_IN), BF16),
        grid=(T // tm, D_IN // tn),
        in_specs=[
            pl.BlockSpec((tm, D_MODEL), lambda i, j: (i, 0)),
            pl.BlockSpec((None, D_MODEL, tn), lambda i, j: (layer, 0, j)),
        ],
        out_specs=pl.BlockSpec((tm, tn), lambda i, j: (i, j)),
        compiler_params=pltpu.CompilerParams(
            dimension_semantics=("arbitrary", "arbitrary"), vmem_limit_bytes=VMEM_LIMIT_BYTES),
        name="proj",
    )(h2, w_bf)


NAT, MOD4 = 0, 1


def _attn_kernel(q_ref, k_ref, v_ref, cos_ref, sin_ref, gq_ref, gk_ref, bias_ref, perm_ref, o_ref,
                 qs, ks, vs, t0, q4, k4, v4, *stats, seq, group):
    obuf, lbuf, mbuf = stats[0:3], stats[3:6], stats[6:9]
    chunk = 256
    quarter = seq // 4

    @pl.when((pl.program_id(0) == 0) & (pl.program_id(1) == 0))
    def _():
        vs[0, :, HEAD_DIM:2 * HEAD_DIM] = jnp.ones((seq, HEAD_DIM), BF16)

    def prepare(src_ref, g_ref, dst, mod4_f32, first_layout, normalize, scale):
        if normalize:
            gain = g_ref[...] * scale
            gain_rot = pltpu.roll(gain, HEAD_DIM // 2, 1)

        def body(i, carry):
            rows = pl.ds(pl.multiple_of(i * chunk, chunk), chunk)
            raw = src_ref[rows, :]
            y = raw.astype(F32)
            if normalize:
                rot = jnp.dot(raw, perm_ref[...], preferred_element_type=F32)
                inv = lax.rsqrt(jnp.mean(y * y, axis=-1, keepdims=True) + EPS)
                y = (y * gain * cos_ref[rows, :] + rot * gain_rot * sin_ref[rows, :]) * inv
            t0[rows, :] = y
            if first_layout == NAT:
                dst[NAT, rows, 0:HEAD_DIM] = y.astype(BF16)
            return carry
        lax.fori_loop(0, seq // chunk, body, 0, unroll=True)

        for r4 in range(4):
            def body4(i, carry, r4=r4):
                y = t0[pl.ds(r4 + i * (4 * chunk), chunk, stride=4), :]
                rows = pl.ds(pl.multiple_of(r4 * quarter + i * chunk, chunk), chunk)
                mod4_f32[rows, :] = y
                dst[MOD4 - first_layout, rows, 0:HEAD_DIM] = y.astype(BF16)
                return carry
            lax.fori_loop(0, quarter // chunk, body4, 0, unroll=True)

    prepare(q_ref, gq_ref, qs, q4, MOD4, True, HEAD_DIM ** -0.5)
    prepare(k_ref, gk_ref, ks, k4, NAT, True, 1.0)
    prepare(v_ref, None, vs, v4, MOD4, False, 1.0)

    n_blocks = seq // Q_BLOCK
    piece = Q_BLOCK // 4
    for br, d in enumerate(DILATIONS):
        nb = n_blocks // d

        def one_block(c, br=br, d=d, nb=nb):
            n = lax.rem(c, nb)
            first = n == 0
            q_row = pl.multiple_of(c * Q_BLOCK, Q_BLOCK)
            k_row = pl.multiple_of(jnp.where(first, c * Q_BLOCK, (c - 1) * Q_BLOCK), Q_BLOCK)
            if d == 1:
                runs = [pl.ds(pl.multiple_of(r4 * quarter + c * piece, piece), piece) for r4 in range(4)]
                qb = jnp.concatenate([qs[0, run, :] for run in runs], axis=0)
            elif d == 4:
                qb = qs[0, pl.ds(q_row, Q_BLOCK), :]
            if d == 16:
                r = c // nb
                base = (r % 4) * quarter + r // 4
                qb = q4[pl.ds(base + n * (4 * Q_BLOCK), Q_BLOCK, stride=4), :].astype(BF16)
                keys = pl.ds(base, 2 * Q_BLOCK, stride=4)
                kb = k4[keys, :].astype(BF16)
                vb = jnp.concatenate([v4[keys, :].astype(BF16), jnp.ones((2 * Q_BLOCK, HEAD_DIM), BF16)], axis=1)
            else:
                kb = ks[br, pl.ds(k_row, 2 * Q_BLOCK), :]
                if d == 1:
                    vb = jnp.concatenate([v_ref[pl.ds(k_row, 2 * Q_BLOCK), :],
                                          jnp.ones((2 * Q_BLOCK, HEAD_DIM), BF16)], axis=1)
                else:
                    vb = vs[0, pl.ds(k_row, 2 * Q_BLOCK), :]
            s = lax.dot_general(qb, kb, (((1,), (1,)), ((), ())), preferred_element_type=F32)
            s = s + bias_ref[jnp.where(first, 0, 1) + (0 if d == 1 else 2)]
            m = jnp.max(s, axis=-1, keepdims=True)
            p = jnp.exp(s - m).astype(BF16)
            oe = jnp.dot(p, vb, preferred_element_type=F32)
            mb = jnp.broadcast_to(m, (Q_BLOCK, HEAD_DIM))
            ob, lb = oe[:, 0:HEAD_DIM], oe[:, HEAD_DIM:2 * HEAD_DIM]
            if d == 1:
                for r4, run in enumerate(runs):
                    sub = slice(r4 * piece, (r4 + 1) * piece)
                    obuf[br][run, :] = ob[sub, :]
                    lbuf[br][run, :] = lb[sub, :]
                    mbuf[br][run, :] = mb[sub, :]
            else:
                if d == 4:
                    rows = pl.ds(q_row, Q_BLOCK)
                else:
                    rows = pl.ds(base + n * (4 * Q_BLOCK), Q_BLOCK, stride=4)
                obuf[br][rows, :] = ob
                lbuf[br][rows, :] = lb
                mbuf[br][rows, :] = mb

        def block_group(it, carry, one_block=one_block):
            for g in range(group):
                one_block(it * group + g)
            return carry
        lax.fori_loop(0, n_blocks // group, block_group, 0)

    for r4 in range(4):
        def finish(i, carry, r4=r4):
            rows = pl.ds(pl.multiple_of(r4 * quarter + i * chunk, chunk), chunk)
            ms = [mbuf[br][rows, :] for br in range(3)]
            m_all = jnp.maximum(jnp.maximum(ms[0], ms[1]), ms[2])
            num = den = None
            for br in range(3):
                w = jnp.exp(ms[br] - m_all)
                t_num = w * obuf[br][rows, :]
                t_den = w * lbuf[br][rows, :]
                num = t_num if num is None else num + t_num
                den = t_den if den is None else den + t_den
            o_ref[pl.ds(r4 + i * (4 * chunk), chunk, stride=4), :] = (num / den).astype(o_ref.dtype)
            return carry
        lax.fori_loop(0, quarter // chunk, finish, 0, unroll=2)


def _attention(proj3, cos2, sin2, gq, gk, bias, *, group=32):
    B, S, _ = proj3.shape
    head_spec = lambda off: pl.BlockSpec((None, S, HEAD_DIM), lambda b, h: (b, 0, off + h))
    const = lambda shape: pl.BlockSpec(shape, lambda b, h: (0,) * len(shape), pipeline_mode=pl.Buffered(1))
    return pl.pallas_call(
        functools.partial(_attn_kernel, seq=S, group=group),
        out_shape=jax.ShapeDtypeStruct((B, S, D_ATT), F32),
        grid=(B, N_HEADS),
        in_specs=[
            head_spec(0), head_spec(N_HEADS), head_spec(2 * N_HEADS),
            const((S, HEAD_DIM)), const((S, HEAD_DIM)),
            const((1, HEAD_DIM)), const((1, HEAD_DIM)),
            const((4, Q_BLOCK, 2 * Q_BLOCK)),
            const((HEAD_DIM, HEAD_DIM)),
        ],
        out_specs=pl.BlockSpec((None, S, HEAD_DIM), lambda b, h: (b, 0, h)),
        scratch_shapes=[
            pltpu.VMEM((1, S, HEAD_DIM), BF16), pltpu.VMEM((2, S, HEAD_DIM), BF16),
            pltpu.VMEM((1, S, 2 * HEAD_DIM), BF16),
        ] + [pltpu.VMEM((S, HEAD_DIM), F32)] * (4 + 9),
        compiler_params=pltpu.CompilerParams(
            dimension_semantics=("arbitrary", "arbitrary"), vmem_limit_bytes=VMEM_LIMIT_BYTES),
        name="dilated_attention",
    )(proj3, proj3, proj3, cos2, sin2, gq, gk, bias,
      jnp.roll(jnp.eye(HEAD_DIM, dtype=BF16), HEAD_DIM // 2, axis=1))


CONV_HALO = 32


def _conv_kernel(a_ref, b_ref, dw_ref, dwb_ref, lng_ref, lnb_ref, o_ref, u_ext, y_s, *, row_chunk, conv_rows):
    tm = a_ref.shape[0]

    def row_loop(body, n_rows):
        def step(i, carry):
            body(pl.ds(pl.multiple_of(i * n_rows, n_rows), n_rows),
                 pl.ds(pl.multiple_of(i * n_rows + CONV_HALO, SUBLANES), n_rows))
            return carry
        lax.fori_loop(0, tm // n_rows, step, 0)

    @pl.when(pl.program_id(1) == 0)
    def _():
        u_ext[0:CONV_HALO, :] = jnp.zeros((CONV_HALO, D_CONV), F32)

    def glu(rows, ext_rows):
        u_ext[ext_rows, :] = a_ref[rows, :].astype(F32) * _sigmoid(b_ref[rows, :].astype(F32))
    row_loop(glu, 2 * row_chunk)

    lead = CONV_HALO - (CONV_WIDTH - 1)

    def lanes_body(cc, carry):
        lanes = pl.ds(pl.multiple_of(cc * 128, 128), 128)
        for rc in range(tm // conv_rows):
            r0 = rc * conv_rows
            acc = jnp.broadcast_to(dwb_ref[:, lanes], (conv_rows, 128))
            for shift in range(SUBLANES):
                n_rows = conv_rows + (SUBLANES if shift else 0)
                part = None
                for off in range(shift, lead + CONV_WIDTH, SUBLANES):
                    if off < lead:
                        continue
                    start = r0 + off - shift
                    term = dw_ref[off - lead:off - lead + 1, lanes] * u_ext[start:start + n_rows, lanes]
                    part = term if part is None else part + term
                acc = acc + part[shift:shift + conv_rows, :]
            y_s[r0:r0 + conv_rows, lanes] = acc
        return carry
    lax.fori_loop(0, D_CONV // 128, lanes_body, 0)

    u_ext[0:CONV_HALO, :] = u_ext[tm:tm + CONV_HALO, :]

    def layer_norm_swish(rows, ext_rows):
        y = y_s[rows, :]
        yc = y - jnp.mean(y, axis=-1, keepdims=True)
        yn = yc * lax.rsqrt(jnp.mean(yc * yc, axis=-1, keepdims=True) + EPS)
        o_ref[rows, :] = _silu(yn * lng_ref[...] + lnb_ref[...]).astype(o_ref.dtype)
    row_loop(layer_norm_swish, 2 * row_chunk)


def _conv_module(proj3, dw, dwb, lng, lnb, *, tm=512):
    B, S, _ = proj3.shape
    col = lambda blk: pl.BlockSpec((None, tm, D_CONV), lambda b, i: (b, i, blk))
    const = lambda shape: pl.BlockSpec(shape, lambda b, i: (0,) * len(shape), pipeline_mode=pl.Buffered(1))
    base = 4 * D_ATT // D_CONV
    return pl.pallas_call(
        functools.partial(_conv_kernel, row_chunk=64, conv_rows=128),
        out_shape=jax.ShapeDtypeStruct((B, S, D_CONV), BF16),
        grid=(B, S // tm),
        in_specs=[
            col(base), col(base + 1),
            const((CONV_WIDTH, D_CONV)), const((1, D_CONV)), const((1, D_CONV)), const((1, D_CONV)),
        ],
        out_specs=pl.BlockSpec((None, tm, D_CONV), lambda b, i: (b, i, 0)),
        scratch_shapes=[pltpu.VMEM((tm + CONV_HALO, D_CONV), F32), pltpu.VMEM((tm, D_CONV), F32)],
        compiler_params=pltpu.CompilerParams(
            dimension_semantics=("arbitrary", "arbitrary"), vmem_limit_bytes=VMEM_LIMIT_BYTES),
        name="conformer_conv",
    )(proj3, proj3, dw, dwb, lng, lnb)


def _out_kernel(att_ref, ga_ref, z_ref, gc_ref, x_ref, g_ref, og_ref, wpw_ref, w_ref, *rest, row_chunk, emit_h):
    if emit_h:
        gn_ref, o_ref, h_ref, lhs, y_pw = rest
    else:
        o_ref, lhs, y_pw = rest
    tm = att_ref.shape[0]
    chunks = [pl.ds(i * row_chunk, row_chunk) for i in range(tm // row_chunk)]

    y_pw[...] = jnp.dot(z_ref[...], wpw_ref[...], preferred_element_type=F32)

    for rows in chunks:
        a = att_ref[rows, :]
        an = a * lax.rsqrt(jnp.mean(a * a, axis=-1, keepdims=True) + EPS) * g_ref[...]
        lhs[rows, 0:D_ATT] = (an * _silu(ga_ref[rows, :].astype(F32))).astype(BF16)

    for rows in chunks:
        c = y_pw[rows, :]
        cn = c * lax.rsqrt(jnp.mean(c * c, axis=-1, keepdims=True) + EPS) * og_ref[...]
        lhs[rows, D_ATT:D_MIX] = (cn * _silu(gc_ref[rows, :].astype(F32))).astype(BF16)

    o_ref[...] = x_ref[...] + jnp.dot(lhs[...], w_ref[...], preferred_element_type=F32)

    if emit_h:
        for rows in chunks:
            xn = o_ref[rows, :]
            hn = xn * lax.rsqrt(jnp.mean(xn * xn, axis=-1, keepdims=True) + EPS)
            h_ref[rows, :] = (hn * gn_ref[...]).astype(BF16)


def _out_proj(att2, proj2, z2, x2, g, og, wpw_bf, w_bf, layer, g_next=None, *, tm=256):
    T = x2.shape[0]
    emit_h = g_next is not None
    gate_blk = 3 * D_ATT // D_ATT
    conv_gate_blk = (4 * D_ATT + 2 * D_CONV) // D_CONV
    const = lambda shape: pl.BlockSpec(shape, lambda i: (0,) * len(shape), pipeline_mode=pl.Buffered(1))
    tile = lambda blk: pl.BlockSpec((tm, D_MODEL), lambda i: (i, blk))
    return pl.pallas_call(
        functools.partial(_out_kernel, row_chunk=16, emit_h=emit_h),
        out_shape=[jax.ShapeDtypeStruct((T, D_MODEL), F32)] + [jax.ShapeDtypeStruct((T, D_MODEL), BF16)] * emit_h,
        grid=(T // tm,),
        in_specs=[
            tile(0), tile(gate_blk), tile(0), tile(conv_gate_blk), tile(0),
            const((1, D_ATT)), const((1, D_CONV)),
            pl.BlockSpec((None, D_CONV, D_CONV), lambda i: (layer, 0, 0), pipeline_mode=pl.Buffered(1)),
            pl.BlockSpec((None, D_MIX, D_MODEL), lambda i: (layer, 0, 0), pipeline_mode=pl.Buffered(1)),
        ] + [const((1, D_MODEL))] * emit_h,
        out_specs=[tile(0)] * (1 + emit_h),
        scratch_shapes=[pltpu.VMEM((tm, D_MIX), BF16), pltpu.VMEM((tm, D_CONV), F32)],
        compiler_params=pltpu.CompilerParams(
            dimension_semantics=("arbitrary",), vmem_limit_bytes=VMEM_LIMIT_BYTES),
        name="out_proj",
    )(att2, proj2, z2, proj2, x2, g, og, wpw_bf, w_bf, *([g_next] if emit_h else []))


def _rope_tables(seq):
    inv_freq = 1.0 / (ROPE_THETA ** (jnp.arange(0, HEAD_DIM, 2, dtype=F32) / HEAD_DIM))
    ang = jnp.arange(seq, dtype=F32)[:, None] * inv_freq[None, :]
    cos, sin = jnp.cos(ang), jnp.sin(ang)
    return jnp.concatenate([cos, cos], axis=-1), jnp.concatenate([-sin, sin], axis=-1)


def _mask_bias():
    rows = jnp.arange(Q_BLOCK)
    rows_mod4 = 4 * (rows % (Q_BLOCK // 4)) + rows // (Q_BLOCK // 4)
    ik = jnp.arange(2 * Q_BLOCK)[None, :]
    def bias(iq, shift):
        dist = iq[:, None] + shift - ik
        return jnp.where((dist >= 0) & (dist <= Q_BLOCK), 0.0, MASKED).astype(F32)
    return jnp.stack([bias(rows_mod4, 0), bias(rows_mod4, Q_BLOCK), bias(rows, 0), bias(rows, Q_BLOCK)])


def kernel(x, norm_g, w_in, q_norm_g, k_norm_g, dw_kernel, dw_bias, conv_ln_g, conv_ln_b, w_pw,
           att_out_g, conv_out_g, w_out):
    B, S, D = x.shape
    assert D == D_MODEL and S % (16 * 2 * Q_BLOCK) == 0
    depth = norm_g.shape[0]
    cos2, sin2 = _rope_tables(S)
    bias = _mask_bias()
    row = lambda v: v.reshape(1, -1)
    x2 = x.reshape(B * S, D)
    w_in_bf, w_pw_bf, w_out_bf = w_in.astype(BF16), w_pw.astype(BF16), w_out.astype(BF16)
    h2 = None
    for layer in range(depth):
        if h2 is None:
            proj2 = _norm_proj(x2, row(norm_g[layer]), w_in_bf, layer)
        else:
            proj2 = _proj(h2, w_in_bf, layer)
        proj3 = proj2.reshape(B, S, D_IN)
        att = _attention(proj3, cos2, sin2, row(q_norm_g[layer]), row(k_norm_g[layer]), bias)
        z = _conv_module(proj3, dw_kernel[layer], row(dw_bias[layer]), row(conv_ln_g[layer]),
                         row(conv_ln_b[layer]))
        g_next = row(norm_g[layer + 1]) if layer + 1 < depth else None
        outs = _out_proj(att.reshape(B * S, D_ATT), proj2, z.reshape(B * S, D_CONV), x2,
                         row(att_out_g[layer]), row(conv_out_g[layer]), w_pw_bf, w_out_bf, layer, g_next)
        x2, h2 = outs if g_next is not None else (outs[0], None)
    return x2.reshape(B, S, D)
```
